```python
import math
import jax, jax.numpy as jnp
from jax import lax
import numpy as np

D_MODEL = 2048
BATCH = 1
SEQ = 16384
DEPTH = 1

N_HEADS = D_MODEL // 128
N_KV_HEADS = 4
HEAD_DIM = 64
GROUP = N_HEADS // N_KV_HEADS
WINDOW = 128
BLOCK = 128
ROPE_THETA = 10000.0
CONV_CH = D_MODEL // 4
CONV_WIDTH = 31
MEM_TOKENS = 256
MEM_HEADS = 4
MEM_HEAD_DIM = D_MODEL // (4 * MEM_HEADS)
ATTN_Q_W = N_HEADS * HEAD_DIM
KV_W = N_KV_HEADS * HEAD_DIM
MEM_W = MEM_HEADS * MEM_HEAD_DIM
N_BRANCH = 3
IN_W = ATTN_Q_W + 2 * KV_W + 2 * CONV_CH + MEM_W + N_BRANCH * D_MODEL
PEER_HEADS = 8
N_KEYS = 128
N_EXPERTS = N_KEYS * N_KEYS
PEER_TOPK = 16
PEER_QDIM = 256
PEER_HALF = PEER_QDIM // 2
PEER_CHUNK = 128
EPS = 1e-6
NEG = -1e30

kernel_name = "hybrid_swa_conformer_mem_peer_block"


def rms_norm(x, g):
    xf = x.astype(jnp.float32)
    y = xf * lax.rsqrt(jnp.mean(xf * xf, axis=-1, keepdims=True) + EPS)
    return (y * g.astype(jnp.float32)).astype(x.dtype)


def layer_norm(x, g, b):
    xf = x.astype(jnp.float32)
    mu = jnp.mean(xf, axis=-1, keepdims=True)
    xc = xf - mu
    y = xc * lax.rsqrt(jnp.mean(xc * xc, axis=-1, keepdims=True) + EPS)
    return (y * g.astype(jnp.float32) + b.astype(jnp.float32)).astype(x.dtype)


def rotary(x, cos, sin):
    xf = x.astype(jnp.float32)
    x1, x2 = jnp.split(xf, 2, axis=-1)
    return jnp.concatenate([x1 * cos - x2 * sin, x2 * cos + x1 * sin], axis=-1).astype(x.dtype)


def sliding_window_attention(q, k, v, sinks):
    B, S = q.shape[0], q.shape[1]
    nb = S // BLOCK
    q = q.reshape(B, nb, BLOCK, N_KV_HEADS, GROUP, HEAD_DIM)
    k = k.reshape(B, nb, BLOCK, N_KV_HEADS, HEAD_DIM)
    v = v.reshape(B, nb, BLOCK, N_KV_HEADS, HEAD_DIM)

    def with_prev(t):
        prev = jnp.concatenate([jnp.zeros_like(t[:, :1]), t[:, :-1]], axis=1)
        return jnp.concatenate([prev, t], axis=2)

    kb, vb = with_prev(k), with_prev(v)
    s = jnp.einsum('bnqhgd,bnkhd->bnhgqk', q, kb).astype(jnp.float32) * (HEAD_DIM ** -0.5)
    qi = jnp.arange(BLOCK)[:, None]
    kj = jnp.arange(2 * BLOCK)[None, :]
    diff = qi + BLOCK - kj
    band = (diff >= 0) & (diff < WINDOW)
    blk = jnp.arange(nb)[:, None, None]
    mask = band[None] & ((blk > 0) | (kj >= BLOCK)[None])
    s = jnp.where(mask[None, :, None, None], s, NEG)
    sink = jnp.broadcast_to(sinks.astype(jnp.float32).reshape(N_KV_HEADS, GROUP)[None, None, :, :, None, None],
                            s.shape[:-1] + (1,))
    p = jax.nn.softmax(jnp.concatenate([s, sink], axis=-1), axis=-1)[..., :-1].astype(v.dtype)
    o = jnp.einsum('bnhgqk,bnkhd->bnqhgd', p, vb)
    return o.reshape(B, S, N_HEADS * HEAD_DIM)


def conformer_conv(u, dw_w, dw_b, ln_g, ln_b, w_o):
    a, b = jnp.split(u, 2, axis=-1)
    glu = a * jax.nn.sigmoid(b)
    c = lax.conv_general_dilated(glu, dw_w, window_strides=(1,), padding=((CONV_WIDTH - 1, 0),),
                                 dimension_numbers=('NWC', 'WIO', 'NWC'),
                                 feature_group_count=CONV_CH) + dw_b
    c = layer_norm(c, ln_g, ln_b)
    return jax.nn.silu(c) @ w_o


def memory_attention(mq, mem, g_mem, w_mem_kv, mq_g, mk_g, w_o):
    B, S = mq.shape[0], mq.shape[1]
    mq = rms_norm(mq.reshape(B, S, MEM_HEADS, MEM_HEAD_DIM), mq_g)
    kv = rms_norm(mem, g_mem) @ w_mem_kv
    mk, mv = jnp.split(kv, 2, axis=-1)
    mk = rms_norm(mk.reshape(B, -1, MEM_HEADS, MEM_HEAD_DIM), mk_g)
    mv = mv.reshape(B, -1, MEM_HEADS, MEM_HEAD_DIM)
    s = jnp.einsum('bshd,bmhd->bhsm', mq, mk).astype(jnp.float32) * (MEM_HEAD_DIM ** -0.5)
    p = jax.nn.softmax(s, axis=-1).astype(mv.dtype)
    o = jnp.einsum('bhsm,bmhd->bshd', p, mv).reshape(B, S, MEM_W)
    return o @ w_o


def peer(h, w_query, sub_keys, expert_u, expert_v):
    B, S, D = h.shape
    q = (h @ w_query).reshape(B, S, PEER_HEADS, 2, PEER_HALF)
    sc = jnp.einsum('bshpd,hpnd->bshpn', q, sub_keys).astype(jnp.float32)
    vals, idx = lax.top_k(sc, PEER_TOPK)
    cand_s = (vals[..., 0, :, None] + vals[..., 1, None, :]).reshape(B, S, PEER_HEADS, PEER_TOPK * PEER_TOPK)
    cand_i = (idx[..., 0, :, None] * N_KEYS + idx[..., 1, None, :]).reshape(B, S, PEER_HEADS, PEER_TOPK * PEER_TOPK)
    top_s, pos = lax.top_k(cand_s, PEER_TOPK)
    experts = jnp.take_along_axis(cand_i, pos, axis=-1)
    gates = jax.nn.softmax(top_s, axis=-1).astype(h.dtype)
    n_chunks = (B * S) // PEER_CHUNK
    hf = h.reshape(n_chunks, PEER_CHUNK, D)
    ef = experts.reshape(n_chunks, PEER_CHUNK, PEER_HEADS * PEER_TOPK)
    gf = gates.reshape(n_chunks, PEER_CHUNK, PEER_HEADS * PEER_TOPK)

    def block(args):
        hc, ec, gc = args
        a = jnp.einsum('cd,ckd->ck', hc, expert_u[ec])
        act = jax.nn.gelu(a, approximate=False) * gc
        return jnp.einsum('ck,ckd->cd', act, expert_v[ec])

    return lax.map(block, (hf, ef, gf)).reshape(B, S, D)


def setup_inputs(seed: int = 0) -> dict:
    key = jax.random.key(seed)
    ks = jax.random.split(key, 32)
    L = DEPTH

    def nrm(k, shape, scale):
        return jax.random.normal(k, shape, jnp.float32) * scale

    return {
        "x": nrm(ks[0], (BATCH, SEQ, D_MODEL), 1.0),
        "mem": nrm(ks[1], (BATCH, MEM_TOKENS, D_MODEL), 1.0),
        "positions": jnp.tile(jnp.arange(SEQ, dtype=jnp.int32)[None], (BATCH, 1)),
        "g_mix": 1.0 + nrm(ks[2], (L, D_MODEL), 0.02),
        "w_in": nrm(ks[3], (L, D_MODEL, IN_W), D_MODEL ** -0.5),
        "q_norm_g": 1.0 + nrm(ks[4], (L, HEAD_DIM), 0.02),
        "k_norm_g": 1.0 + nrm(ks[5], (L, HEAD_DIM), 0.02),
        "attn_sinks": nrm(ks[6], (L, N_HEADS), 0.5),
        "w_attn_o": nrm(ks[7], (L, ATTN_Q_W, D_MODEL), ATTN_Q_W ** -0.5),
        "conv_dw_w": nrm(ks[8], (L, CONV_WIDTH, 1, CONV_CH), CONV_WIDTH ** -0.5),
        "conv_dw_b": nrm(ks[9], (L, CONV_CH), 0.02),
        "conv_ln_g": 1.0 + nrm(ks[10], (L, CONV_CH), 0.02),
        "conv_ln_b": nrm(ks[11], (L, CONV_CH), 0.02),
        "w_conv_o": nrm(ks[12], (L, CONV_CH, D_MODEL), CONV_CH ** -0.5),
        "g_mem": 1.0 + nrm(ks[13], (L, D_MODEL), 0.02),
        "w_mem_kv": nrm(ks[14], (L, D_MODEL, 2 * MEM_W), D_MODEL ** -0.5),
        "mq_norm_g": 1.0 + nrm(ks[15], (L, MEM_HEAD_DIM), 0.02),
        "mk_norm_g": 1.0 + nrm(ks[16], (L, MEM_HEAD_DIM), 0.02),
        "w_mem_o": nrm(ks[17], (L, MEM_W, D_MODEL), MEM_W ** -0.5),
        "w_out": nrm(ks[18], (L, D_MODEL, D_MODEL), D_MODEL ** -0.5),
        "g_ffn": 1.0 + nrm(ks[19], (L, D_MODEL), 0.02),
        "w_query": nrm(ks[20], (L, D_MODEL, PEER_HEADS * PEER_QDIM), D_MODEL ** -0.5),
        "sub_keys": nrm(ks[21], (L, PEER_HEADS, 2, N_KEYS, PEER_HALF), PEER_HALF ** -0.5),
        "expert_u": nrm(ks[22], (L, N_EXPERTS, D_MODEL), D_MODEL ** -0.5),
        "expert_v": nrm(ks[23], (L, N_EXPERTS, D_MODEL), (PEER_HEADS * PEER_TOPK) ** -0.5),
    }


def reference(x, mem, positions, g_mix, w_in, q_norm_g, k_norm_g, attn_sinks, w_attn_o,
              conv_dw_w, conv_dw_b, conv_ln_g, conv_ln_b, w_conv_o, g_mem, w_mem_kv,
              mq_norm_g, mk_norm_g, w_mem_o, w_out, g_ffn, w_query, sub_keys, expert_u, expert_v):
    B, S, _ = x.shape
    inv_freq = ROPE_THETA ** (-jnp.arange(0, HEAD_DIM, 2, dtype=jnp.float32) / HEAD_DIM)
    ang = positions.astype(jnp.float32)[..., None] * inv_freq
    cos, sin = jnp.cos(ang)[:, :, None, :], jnp.sin(ang)[:, :, None, :]
    splits = np.cumsum([ATTN_Q_W, KV_W, KV_W, 2 * CONV_CH, MEM_W, D_MODEL, D_MODEL]).tolist()

    for l in range(DEPTH):
        h = rms_norm(x, g_mix[l])
        proj = h @ w_in[l]
        q, k, v, conv_in, mq, ga, gc, gm = jnp.split(proj, splits, axis=-1)
        q = rotary(rms_norm(q.reshape(B, S, N_HEADS, HEAD_DIM), q_norm_g[l]), cos, sin)
        k = rotary(rms_norm(k.reshape(B, S, N_KV_HEADS, HEAD_DIM), k_norm_g[l]), cos, sin)
        v = v.reshape(B, S, N_KV_HEADS, HEAD_DIM)
        attn = sliding_window_attention(q, k, v, attn_sinks[l]) @ w_attn_o[l]
        conv = conformer_conv(conv_in, conv_dw_w[l], conv_dw_b[l], conv_ln_g[l], conv_ln_b[l], w_conv_o[l])
        memo = memory_attention(mq, mem, g_mem[l], w_mem_kv[l], mq_norm_g[l], mk_norm_g[l], w_mem_o[l])
        merged = jax.nn.sigmoid(ga) * attn + jax.nn.sigmoid(gc) * conv + jax.nn.sigmoid(gm) * memo
        x = x + merged @ w_out[l]
        x = x + peer(rms_norm(x, g_ffn[l]), w_query[l], sub_keys[l], expert_u[l], expert_v[l])
    return x
```

```python
import functools

import jax
import jax.numpy as jnp
from jax import lax
from jax.experimental import pallas as pl
from jax.experimental.pallas import tpu as pltpu

F32 = jnp.float32
BF16 = jnp.bfloat16

EPS = 1e-6
NEG = -1e30
LANES = 128
HEAD_DIM = 64
N_HEADS = 16
N_KV_HEADS = 4
ATTN_BLOCK = 128
ROPE_THETA = 10000.0
CONV_WIDTH = 31
CONV_HALO = 32
MEM_HEADS = 4
MEM_HEAD_DIM = 128
PEER_HEADS = 8
N_KEYS = 128
PEER_TOPK = 16
UNRANKED = 64.0

VMEM_LIMIT = 56 * 1024 * 1024


def _params(n_axes, vmem=VMEM_LIMIT):
    return pltpu.CompilerParams(
        dimension_semantics=("arbitrary",) * n_axes, vmem_limit_bytes=vmem)


def _in_proj_body(x_ref, g_ref, w_ref, o_ref, h_ref):
    @pl.when(pl.program_id(1) == 0)
    def _():
        x = x_ref[...]
        ms = jnp.mean(x * x, axis=-1, keepdims=True)
        h_ref[...] = (x * lax.rsqrt(ms + EPS) * g_ref[...]).astype(BF16)

    o_ref[...] = jnp.dot(h_ref[...], w_ref[...],
                         preferred_element_type=F32).astype(o_ref.dtype)


def _in_proj(x, g, w, tm, tn):
    s, d = x.shape
    n = w.shape[1]
    return pl.pallas_call(
        _in_proj_body,
        grid=(s // tm, n // tn),
        in_specs=[pl.BlockSpec((tm, d), lambda i, j: (i, 0)),
                  pl.BlockSpec((1, d), lambda i, j: (0, 0)),
                  pl.BlockSpec((d, tn), lambda i, j: (0, j))],
        out_specs=pl.BlockSpec((tm, tn), lambda i, j: (i, j)),
        out_shape=jax.ShapeDtypeStruct((s, n), BF16),
        scratch_shapes=[pltpu.VMEM((tm, d), BF16)],
        compiler_params=_params(2),
        name="in_proj",
    )(x, g, w)


def _swa_body(sinks_ref, q_ref, kv_ref, kvp_ref, pos_ref, posp_ref, invf_ref, gq_ref,
              gk_ref, bd_ref, o_ref, klo_ref, khi_ref, vlo_ref, vhi_ref, *, tq):
    i = pl.program_id(0)
    nblk = tq // ATTN_BLOCK
    lane = lax.broadcasted_iota(jnp.int32, (1, LANES), 1)
    first_half = (lane % HEAD_DIM) < (HEAD_DIM // 2)
    low_head = lane < HEAD_DIM
    invf = invf_ref[...]
    bd = bd_ref[...]

    def rope_tables(pos):
        ang = pos * invf
        sin = jnp.sin(ang)
        return jnp.cos(ang), jnp.where(first_half, -sin, sin)

    def norm_rope(xf, g, cos, sin_signed):
        ms = jnp.dot((xf * xf).astype(BF16), bd, preferred_element_type=F32)
        xn = xf * lax.rsqrt(ms + EPS) * g
        swapped = jnp.where(first_half, pltpu.roll(xn, LANES - HEAD_DIM // 2, 1),
                            pltpu.roll(xn, HEAD_DIM // 2, 1))
        return xn * cos + swapped * sin_signed

    cos_c, sin_c = rope_tables(pos_ref[...])
    cos_p, sin_p = rope_tables(posp_ref[...])
    gk = gk_ref[...]

    def stage_kv(kv_tile, cos, sin_signed, row0, rows):
        for kc in range(N_KV_HEADS // 2):
            kf = kv_tile[:, kc * LANES:(kc + 1) * LANES].astype(F32)
            kr = norm_rope(kf, gk, cos, sin_signed)
            vf = kv_tile[:, (2 + kc) * LANES:(3 + kc) * LANES].astype(F32)
            for src, lo_ref, hi_ref in ((kr, klo_ref, khi_ref), (vf, vlo_ref, vhi_ref)):
                rolled = pltpu.roll(src, HEAD_DIM, 1)
                zero = jnp.zeros_like(src)
                sl = pl.ds(row0, rows)
                lo_ref[2 * kc, sl, :] = jnp.where(low_head, src, zero).astype(BF16)
                hi_ref[2 * kc, sl, :] = jnp.where(low_head, zero, rolled).astype(BF16)
                lo_ref[2 * kc + 1, sl, :] = jnp.where(low_head, rolled, zero).astype(BF16)
                hi_ref[2 * kc + 1, sl, :] = jnp.where(low_head, zero, src).astype(BF16)

    stage_kv(kvp_ref[...], cos_p, sin_p, 0, ATTN_BLOCK)
    stage_kv(kv_ref[...], cos_c, sin_c, ATTN_BLOCK, tq)

    qi = lax.broadcasted_iota(jnp.int32, (ATTN_BLOCK, 2 * ATTN_BLOCK), 0)
    kj = lax.broadcasted_iota(jnp.int32, (ATTN_BLOCK, 2 * ATTN_BLOCK), 1)
    diff = qi + ATTN_BLOCK - kj
    band = (diff >= 0) & (diff < ATTN_BLOCK)
    gq = gq_ref[...]
    nt = (((1,), (1,)), ((), ()))

    for pc in range(N_HEADS // 2):
        g = pc // 2
        qf = q_ref[:, pc * LANES:(pc + 1) * LANES].astype(F32)
        qr = (norm_rope(qf, gq, cos_c, sin_c) * (HEAD_DIM ** -0.5)).astype(BF16)
        for n in range(nblk):
            first_key = jnp.where((i * nblk + n) > 0, 0, ATTN_BLOCK)
            mask = band & (kj >= first_key)
            qn = qr[n * ATTN_BLOCK:(n + 1) * ATTN_BLOCK]
            keys = pl.ds(n * ATTN_BLOCK, 2 * ATTN_BLOCK)
            out = None
            for hh, (k_ref, v_ref) in enumerate(((klo_ref, vlo_ref), (khi_ref, vhi_ref))):
                s = lax.dot_general(qn, k_ref[g, keys, :], nt, preferred_element_type=F32)
                s = jnp.where(mask, s, NEG)
                sink = sinks_ref[2 * pc + hh]
                m = jnp.maximum(jnp.max(s, axis=-1, keepdims=True), sink)
                e = jnp.exp(s - m)
                den = jnp.sum(e, axis=-1, keepdims=True) + jnp.exp(sink - m)
                p = (e / den).astype(BF16)
                o = jnp.dot(p, v_ref[g, keys, :], preferred_element_type=F32)
                out = o if out is None else out + o
            o_ref[n * ATTN_BLOCK:(n + 1) * ATTN_BLOCK, pc * LANES:(pc + 1) * LANES] = (
                out.astype(o_ref.dtype))


def _swa(proj, pos_b, invf, gq, gk, bd, sinks, tq):
    s = proj.shape[0]
    rpb = tq // ATTN_BLOCK
    kv_blk = (N_HEADS * HEAD_DIM) // (2 * N_KV_HEADS * HEAD_DIM)
    prev = lambda i: jnp.maximum(i * rpb - 1, 0)
    rows = tq + ATTN_BLOCK
    stage = pltpu.VMEM((N_KV_HEADS, rows, LANES), BF16)
    return pl.pallas_call(
        functools.partial(_swa_body, tq=tq),
        grid=(s // tq,),
        in_specs=[pl.BlockSpec(memory_space=pltpu.SMEM),
                  pl.BlockSpec((tq, N_HEADS * HEAD_DIM), lambda i: (i, 0)),
                  pl.BlockSpec((tq, 2 * N_KV_HEADS * HEAD_DIM), lambda i: (i, kv_blk)),
                  pl.BlockSpec((ATTN_BLOCK, 2 * N_KV_HEADS * HEAD_DIM),
                               lambda i: (prev(i), kv_blk)),
                  pl.BlockSpec((tq, LANES), lambda i: (i, 0)),
                  pl.BlockSpec((ATTN_BLOCK, LANES), lambda i: (prev(i), 0)),
                  pl.BlockSpec((1, LANES), lambda i: (0, 0)),
                  pl.BlockSpec((1, LANES), lambda i: (0, 0)),
                  pl.BlockSpec((1, LANES), lambda i: (0, 0)),
                  pl.BlockSpec((LANES, LANES), lambda i: (0, 0))],
        out_specs=pl.BlockSpec((tq, N_HEADS * HEAD_DIM), lambda i: (i, 0)),
        out_shape=jax.ShapeDtypeStruct((s, N_HEADS * HEAD_DIM), BF16),
        scratch_shapes=[stage, stage, stage, stage],
        compiler_params=_params(1),
        name="swa",
    )(sinks, proj, proj, proj, pos_b, pos_b, invf, gq, gk, bd)


def _conv_body(a_ref, b_ref, ah_ref, bh_ref, w_ref, b0_ref, lg_ref, lb_ref, o_ref,
               ext_ref, c_ref, *, tm):
    i = pl.program_id(0)
    ext_ref[CONV_HALO:, :] = a_ref[...].astype(F32) * jax.nn.sigmoid(b_ref[...].astype(F32))
    halo = ah_ref[...].astype(F32) * jax.nn.sigmoid(bh_ref[...].astype(F32))
    ext_ref[:CONV_HALO, :] = jnp.where(i > 0, halo, jnp.zeros_like(halo))
    ch = ext_ref.shape[1]
    rows = 128
    for cc in range(ch // LANES):
        cols = slice(cc * LANES, (cc + 1) * LANES)
        for rc in range(tm // rows):
            acc = jnp.broadcast_to(b0_ref[:, cols], (rows, LANES))
            for w in range(CONV_WIDTH):
                start = rc * rows + CONV_HALO - (CONV_WIDTH - 1) + w
                acc = acc + ext_ref[start:start + rows, cols] * w_ref[w:w + 1, cols]
            c_ref[rc * rows:(rc + 1) * rows, cols] = acc
    c = c_ref[...]
    mu = jnp.mean(c, axis=-1, keepdims=True)
    xc = c - mu
    y = xc * lax.rsqrt(jnp.mean(xc * xc, axis=-1, keepdims=True) + EPS)
    y = y * lg_ref[...] + lb_ref[...]
    o_ref[...] = (y * jax.nn.sigmoid(y)).astype(o_ref.dtype)


def _conv(proj, dw_w, dw_b, ln_g, ln_b, tm, a_col, ch):
    s = proj.shape[0]
    a_blk = a_col // ch
    hpb = tm // CONV_HALO
    prev = lambda i: jnp.maximum(i * hpb - 1, 0)
    vec = pl.BlockSpec((1, ch), lambda i: (0, 0))
    return pl.pallas_call(
        functools.partial(_conv_body, tm=tm),
        grid=(s // tm,),
        in_specs=[pl.BlockSpec((tm, ch), lambda i: (i, a_blk)),
                  pl.BlockSpec((tm, ch), lambda i: (i, a_blk + 1)),
                  pl.BlockSpec((CONV_HALO, ch), lambda i: (prev(i), a_blk)),
                  pl.BlockSpec((CONV_HALO, ch), lambda i: (prev(i), a_blk + 1)),
                  pl.BlockSpec((CONV_WIDTH, ch), lambda i: (0, 0)),
                  vec, vec, vec],
        out_specs=pl.BlockSpec((tm, ch), lambda i: (i, 0)),
        out_shape=jax.ShapeDtypeStruct((s, ch), BF16),
        scratch_shapes=[pltpu.VMEM((tm + CONV_HALO, ch), F32), pltpu.VMEM((tm, ch), F32)],
        compiler_params=_params(1),
        name="conv",
    )(proj, proj, proj, proj, dw_w, dw_b, ln_g, ln_b)


def _mem_kv_body(mem_ref, g_ref, w_ref, kg_ref, k_ref, v_ref):
    m = mem_ref[...]
    ms = jnp.mean(m * m, axis=-1, keepdims=True)
    h = (m * lax.rsqrt(ms + EPS) * g_ref[...]).astype(BF16)
    kv = jnp.dot(h, w_ref[...], preferred_element_type=F32)
    width = MEM_HEADS * MEM_HEAD_DIM
    for hd in range(MEM_HEADS):
        cols = slice(hd * MEM_HEAD_DIM, (hd + 1) * MEM_HEAD_DIM)
        k = kv[:, cols]
        kms = jnp.mean(k * k, axis=-1, keepdims=True)
        k_ref[:, cols] = (k * lax.rsqrt(kms + EPS) * kg_ref[...]).astype(BF16)
    v_ref[...] = kv[:, width:].astype(BF16)


def _mem_kv(mem, g, w, kg):
    m = mem.shape[0]
    width = MEM_HEADS * MEM_HEAD_DIM
    out = jax.ShapeDtypeStruct((m, width), BF16)
    return pl.pallas_call(
        _mem_kv_body,
        out_shape=(out, out),
        compiler_params=pltpu.CompilerParams(vmem_limit_bytes=VMEM_LIMIT),
        name="mem_kv",
    )(mem, g, w, kg)


def _mem_attn_body(q_ref, k_ref, v_ref, qg_ref, o_ref):
    nt = (((1,), (1,)), ((), ()))
    for hd in range(MEM_HEADS):
        cols = slice(hd * MEM_HEAD_DIM, (hd + 1) * MEM_HEAD_DIM)
        q = q_ref[:, cols].astype(F32)
        qms = jnp.mean(q * q, axis=-1, keepdims=True)
        qn = (q * lax.rsqrt(qms + EPS) * qg_ref[...]).astype(BF16)
        s = lax.dot_general(qn, k_ref[:, cols], nt, preferred_element_type=F32)
        s = s * (MEM_HEAD_DIM ** -0.5)
        e = jnp.exp(s - jnp.max(s, axis=-1, keepdims=True))
        p = (e / jnp.sum(e, axis=-1, keepdims=True)).astype(BF16)
        o_ref[:, cols] = jnp.dot(p, v_ref[:, cols],
                                 preferred_element_type=F32).astype(o_ref.dtype)


def _mem_attn(proj, mk, mv, qg, tm, q_col):
    s = proj.shape[0]
    width = MEM_HEADS * MEM_HEAD_DIM
    m = mk.shape[0]
    return pl.pallas_call(
        _mem_attn_body,
        grid=(s // tm,),
        in_specs=[pl.BlockSpec((tm, width), lambda i: (i, q_col // width)),
                  pl.BlockSpec((m, width), lambda i: (0, 0)),
                  pl.BlockSpec((m, width), lambda i: (0, 0)),
                  pl.BlockSpec((1, MEM_HEAD_DIM), lambda i: (0, 0))],
        out_specs=pl.BlockSpec((tm, width), lambda i: (i, 0)),
        out_shape=jax.ShapeDtypeStruct((s, width), BF16),
        compiler_params=_params(1),
        name="mem_attn",
    )(proj, mk, mv, qg)


def _mix_out_body(x_ref, at_ref, cv_ref, mm_ref, ga0, ga1, gc0, gc1, gm0, gm1,
                  wa_ref, wc_ref, wm_ref, wo_ref, o_ref):
    at = at_ref[...]
    cv = cv_ref[...]
    mm = mm_ref[...]
    acc = x_ref[...]
    half = wo_ref.shape[0] // 2
    for c, (ga, gc, gm) in enumerate(((ga0, gc0, gm0), (ga1, gc1, gm1))):
        cols = slice(c * half, (c + 1) * half)
        merged = jax.nn.sigmoid(ga[...].astype(F32)) * jnp.dot(
            at, wa_ref[:, cols], preferred_element_type=F32)
        merged += jax.nn.sigmoid(gc[...].astype(F32)) * jnp.dot(
            cv, wc_ref[:, cols], preferred_element_type=F32)
        merged += jax.nn.sigmoid(gm[...].astype(F32)) * jnp.dot(
            mm, wm_ref[:, cols], preferred_element_type=F32)
        acc = acc + jnp.dot(merged.astype(BF16), wo_ref[cols, :], preferred_element_type=F32)
    o_ref[...] = acc


def _mix_out(x, attn, conv, memo, proj, wa, wc, wm, wo, tm, gate_col):
    s, d = x.shape
    half = d // 2
    g0 = gate_col // half
    gate = lambda k: pl.BlockSpec((tm, half), lambda i: (i, g0 + k))
    full = lambda a: pl.BlockSpec(a.shape, lambda i: (0, 0))
    return pl.pallas_call(
        _mix_out_body,
        grid=(s // tm,),
        in_specs=[pl.BlockSpec((tm, d), lambda i: (i, 0)),
                  pl.BlockSpec((tm, attn.shape[1]), lambda i: (i, 0)),
                  pl.BlockSpec((tm, conv.shape[1]), lambda i: (i, 0)),
                  pl.BlockSpec((tm, memo.shape[1]), lambda i: (i, 0)),
                  gate(0), gate(1), gate(2), gate(3), gate(4), gate(5),
                  full(wa), full(wc), full(wm), full(wo)],
        out_specs=pl.BlockSpec((tm, d), lambda i: (i, 0)),
        out_shape=jax.ShapeDtypeStruct((s, d), F32),
        compiler_params=_params(1),
        name="mix_out",
    )(x, attn, conv, memo, proj, proj, proj, proj, proj, proj, wa, wc, wm, wo)


def _top16(s):
    rank = jnp.full(s.shape, UNRANKED, F32)
    vals = []
    for r in range(PEER_TOPK):
        m = jnp.max(s, axis=0, keepdims=True)
        sel = s == m
        rank = jnp.where(sel, float(r), rank)
        s = jnp.where(sel, -jnp.inf, s)
        vals.append(m)
    return vals, rank


def _pair_counts(v1, v2):
    shape = (8, v1[0].shape[1])
    row = lax.broadcasted_iota(jnp.int32, shape, 0)
    rowf = row.astype(F32)
    v2lo = jnp.concatenate(v2[:8], axis=0)
    v2hi = jnp.concatenate(v2[8:], axis=0)
    ninf = jnp.full(shape, -jnp.inf, F32)
    groups = [v1[0] + v2lo, v1[0] + v2hi, v1[1] + v2lo]
    index = [rowf, rowf + 8.0, rowf + 16.0]
    for r1 in range(2, 8):
        groups.append(jnp.where(row < PEER_TOPK // (r1 + 1), v1[r1] + v2lo, ninf))
        index.append(rowf + 16.0 * r1)
    groups.append(jnp.concatenate(v1[8:], axis=0) + v2[0])
    index.append(16.0 * (rowf + 8.0))
    cand = jnp.concatenate(groups, axis=0)
    idx = jnp.concatenate(index, axis=0)
    taken = jnp.zeros(cand.shape, F32)
    best = v1[0] + v2[0]
    z = jnp.zeros_like(best)
    for _ in range(PEER_TOPK):
        m = jnp.max(cand, axis=0, keepdims=True)
        first = jnp.min(jnp.where(cand == m, idx, 4096.0), axis=0, keepdims=True)
        sel = idx == first
        cand = jnp.where(sel, -jnp.inf, cand)
        taken = jnp.where(sel, 1.0, taken)
        z = z + jnp.exp(m - best)
    cnt = [jnp.sum(taken[0:16], axis=0, keepdims=True)]
    for r1 in range(1, 8):
        cnt.append(jnp.sum(taken[8 * (r1 + 1):8 * (r1 + 2)], axis=0, keepdims=True))
    cnt += [taken[72 + r:73 + r] for r in range(8)]
    return cnt, z


def _peer_route_body(x_ref, g_ref, wq_ref, keys_ref, h_ref, cnt_ref, e1_ref, rk_ref, e2_ref,
                     q_ref, *, tf):
    x = x_ref[...]
    ms = jnp.mean(x * x, axis=-1, keepdims=True)
    h = (x * lax.rsqrt(ms + EPS) * g_ref[...]).astype(BF16)
    h_ref[...] = h
    nt = (((1,), (1,)), ((), ()))
    q_ref[...] = lax.dot_general(wq_ref[...], h, nt, preferred_element_type=F32).astype(BF16)

    def head(hd, carry):
        q1 = q_ref[pl.ds(pl.multiple_of(hd * 2 * N_KEYS, 2 * N_KEYS), N_KEYS), :]
        q2 = q_ref[pl.ds(pl.multiple_of(hd * 2 * N_KEYS + N_KEYS, N_KEYS), N_KEYS), :]
        s1 = jnp.dot(keys_ref[2 * hd], q1, preferred_element_type=F32)
        s2 = jnp.dot(keys_ref[2 * hd + 1], q2, preferred_element_type=F32)
        for c in range(tf // LANES):
            cols = slice(c * LANES, (c + 1) * LANES)
            s1c = s1[:, cols]
            s2c = s2[:, cols]
            v1, rank1 = _top16(s1c)
            v2, rank2 = _top16(s2c)
            cnt, z = _pair_counts(v1, v2)
            cnt_i = jnp.zeros_like(rank1)
            for r in range(PEER_TOPK):
                cnt_i = jnp.where(rank1 == float(r), cnt[r], cnt_i)
            cnt_ref[hd, :, cols] = cnt_i
            e1_ref[hd, :, cols] = jnp.exp(s1c - v1[0]) / z
            rk_ref[hd, :, cols] = rank2.astype(BF16)
            e2_ref[hd, :, cols] = jnp.exp(s2c - v2[0]).astype(BF16)
        return carry

    lax.fori_loop(0, PEER_HEADS, head, 0)


def _peer_route(x1, g, wq_t, keys, tf):
    s, d = x1.shape
    qw = wq_t.shape[0]
    tab = lambda dt: jax.ShapeDtypeStruct((PEER_HEADS, N_KEYS, s), dt)
    tab_spec = pl.BlockSpec((PEER_HEADS, N_KEYS, tf), lambda i: (0, 0, i))
    return pl.pallas_call(
        functools.partial(_peer_route_body, tf=tf),
        grid=(s // tf,),
        in_specs=[pl.BlockSpec((tf, d), lambda i: (i, 0)),
                  pl.BlockSpec((1, d), lambda i: (0, 0)),
                  pl.BlockSpec((qw, d), lambda i: (0, 0)),
                  pl.BlockSpec(keys.shape, lambda i: (0, 0, 0))],
        out_specs=(pl.BlockSpec((tf, d), lambda i: (i, 0)),
                   tab_spec, tab_spec, tab_spec, tab_spec),
        out_shape=(jax.ShapeDtypeStruct((s, d), BF16), tab(F32), tab(F32), tab(BF16), tab(BF16)),
        scratch_shapes=[pltpu.VMEM((qw, tf), BF16)],
        compiler_params=_params(1),
        name="peer_route",
    )(x1, g, wq_t, keys)


def _peer_experts_body(h_ref, u_ref, vt_ref, cnt_ref, e1_ref, rk_ref, e2_ref, x_ref, o_ref,
                       acc_ref, act_ref, *, rows_per_chunk):
    c = pl.program_id(1)
    nt = (((1,), (1,)), ((), ()))
    tm = h_ref.shape[0]
    a = lax.dot_general(u_ref[...], h_ref[...], nt, preferred_element_type=F32)
    for r in range(rows_per_chunk):
        rows = slice(r * N_KEYS, (r + 1) * N_KEYS)
        ar = a[rows]
        gel = (0.5 * ar * (1.0 + lax.erf(ar * (2.0 ** -0.5)))).astype(BF16)
        gate = None
        for hd in range(PEER_HEADS):
            cnt = jnp.broadcast_to(cnt_ref[hd, r:r + 1, :], (N_KEYS, tm)).astype(BF16)
            e1 = jnp.broadcast_to(e1_ref[hd, r:r + 1, :], (N_KEYS, tm)).astype(BF16)
            e2 = e2_ref[hd]
            term = jnp.where(rk_ref[hd] < cnt, e2, jnp.zeros_like(e2)) * e1
            gate = term if gate is None else gate + term
        act_ref[rows, :] = gel * gate
    contrib = jnp.dot(vt_ref[...], act_ref[...], preferred_element_type=F32)

    @pl.when(c == 0)
    def _():
        acc_ref[...] = contrib

    @pl.when(c > 0)
    def _():
        acc_ref[...] += contrib

    @pl.when(c == pl.num_programs(1) - 1)
    def _():
        o_ref[...] = x_ref[...] + acc_ref[...].T


def _peer_experts(h2, u, vt, cnt, e1, rk, e2, x1, tm, ec):
    s, d = x1.shape
    n_exp = u.shape[0]
    rpc = ec // N_KEYS
    once = pl.Buffered(1)
    tab = pl.BlockSpec((PEER_HEADS, N_KEYS, tm), lambda i, c: (0, 0, i), pipeline_mode=once)
    row = pl.BlockSpec((PEER_HEADS, rpc, tm), lambda i, c: (0, c, i))
    return pl.pallas_call(
        functools.partial(_peer_experts_body, rows_per_chunk=rpc),
        grid=(s // tm, n_exp // ec),
        in_specs=[pl.BlockSpec((tm, d), lambda i, c: (i, 0), pipeline_mode=once),
                  pl.BlockSpec((ec, d), lambda i, c: (c, 0)),
                  pl.BlockSpec((d, ec), lambda i, c: (0, c)),
                  row, row, tab, tab,
                  pl.BlockSpec((tm, d), lambda i, c: (i, 0), pipeline_mode=once)],
        out_specs=pl.BlockSpec((tm, d), lambda i, c: (i, 0)),
        out_shape=jax.ShapeDtypeStruct((s, d), F32),
        scratch_shapes=[pltpu.VMEM((d, tm), F32), pltpu.VMEM((ec, tm), BF16)],
        compiler_params=_params(2),
        name="peer_experts",
    )(h2, u, vt, cnt, e1, rk, e2, x1)


def _tile(s, want):
    return min(s, want)


def kernel(x, mem, positions, g_mix, w_in, q_norm_g, k_norm_g, attn_sinks, w_attn_o,
           conv_dw_w, conv_dw_b, conv_ln_g, conv_ln_b, w_conv_o, g_mem, w_mem_kv,
           mq_norm_g, mk_norm_g, w_mem_o, w_out, g_ffn, w_query, sub_keys, expert_u, expert_v):
    b, s, d = x.shape
    depth = g_mix.shape[0]
    q_w = N_HEADS * HEAD_DIM
    kv_w = N_KV_HEADS * HEAD_DIM
    conv_ch = conv_dw_b.shape[-1]
    mem_w = MEM_HEADS * MEM_HEAD_DIM
    conv_col = q_w + 2 * kv_w
    mq_col = conv_col + 2 * conv_ch
    gate_col = mq_col + mem_w

    row = lambda v: v.reshape(1, -1).astype(F32)
    inv_freq = ROPE_THETA ** (-jnp.arange(0, HEAD_DIM, 2, dtype=F32) / HEAD_DIM)
    invf = jnp.tile(inv_freq, LANES // (HEAD_DIM // 2)).reshape(1, LANES)
    head_of_lane = jnp.arange(LANES) // HEAD_DIM
    bd = (head_of_lane[:, None] == head_of_lane[None, :]).astype(BF16) * (1.0 / HEAD_DIM)

    outs = []
    for bi in range(b):
        xb = x[bi]
        pos_b = jnp.broadcast_to(positions[bi].astype(F32)[:, None], (s, LANES))
        for l in range(depth):
            proj = _in_proj(xb, row(g_mix[l]), w_in[l].astype(BF16), _tile(s, 1024), 1024)
            attn = _swa(proj, pos_b, invf, jnp.tile(row(q_norm_g[l]), (1, 2)),
                        jnp.tile(row(k_norm_g[l]), (1, 2)), bd, attn_sinks[l].astype(F32),
                        _tile(s, 512))
            conv = _conv(proj, conv_dw_w[l].reshape(CONV_WIDTH, conv_ch), row(conv_dw_b[l]),
                         row(conv_ln_g[l]), row(conv_ln_b[l]), _tile(s, 512), conv_col, conv_ch)
            mk, mv = _mem_kv(mem[bi], row(g_mem[l]), w_mem_kv[l].astype(BF16),
                             row(mk_norm_g[l]))
            memo = _mem_attn(proj, mk, mv, row(mq_norm_g[l]), _tile(s, 512), mq_col)
            x1 = _mix_out(xb, attn, conv, memo, proj, w_attn_o[l].astype(BF16),
                          w_conv_o[l].astype(BF16), w_mem_o[l].astype(BF16),
                          w_out[l].astype(BF16), _tile(s, 512), gate_col)
            keys = sub_keys[l].reshape(2 * PEER_HEADS, N_KEYS, -1).astype(BF16)
            h2, cnt, e1, rk, e2 = _peer_route(x1, row(g_ffn[l]), w_query[l].T.astype(BF16),
                                              keys, _tile(s, 256))
            xb = _peer_experts(h2, expert_u[l].astype(BF16), expert_v[l].T.astype(BF16),
                               cnt, e1, rk, e2, x1, _tile(s, 512), 1024)
        outs.append(xb)
    return jnp.stack(outs)
```

```python
import functools

import jax
import jax.numpy as jnp
from jax import lax
from jax.experimental import pallas as pl
from jax.experimental.pallas import tpu as pltpu

F32 = jnp.float32
BF16 = jnp.bfloat16

EPS = 1e-6
NEG = -1e30
LANES = 128
HEAD_DIM = 64
N_HEADS = 16
N_KV_HEADS = 4
ATTN_BLOCK = 128
ROPE_THETA = 10000.0
CONV_WIDTH = 31
CONV_HALO = 32
MEM_HEADS = 4
MEM_HEAD_DIM = 128
PEER_HEADS = 8
N_KEYS = 128
PEER_TOPK = 16
UNRANKED = 64.0
RANK_BASE = 2.0 ** 100
RANK_STEP = 2.0 ** 77

VMEM_LIMIT = 56 * 1024 * 1024


def _params(n_axes, vmem=VMEM_LIMIT, flags=None):
    return pltpu.CompilerParams(
        dimension_semantics=("arbitrary",) * n_axes, vmem_limit_bytes=vmem, flags=flags)


def _in_proj_body(x_ref, g_ref, w_ref, o_ref, h_ref):
    @pl.when(pl.program_id(1) == 0)
    def _():
        x = x_ref[...]
        ms = jnp.mean(x * x, axis=-1, keepdims=True)
        h_ref[...] = (x * lax.rsqrt(ms + EPS) * g_ref[...]).astype(BF16)

    o_ref[...] = jnp.dot(h_ref[...], w_ref[...],
                         preferred_element_type=F32).astype(o_ref.dtype)


def _in_proj(x, g, w, tm, tn):
    s, d = x.shape
    n = w.shape[1]
    return pl.pallas_call(
        _in_proj_body,
        grid=(s // tm, n // tn),
        in_specs=[pl.BlockSpec((tm, d), lambda i, j: (i, 0)),
                  pl.BlockSpec((1, d), lambda i, j: (0, 0)),
                  pl.BlockSpec((d, tn), lambda i, j: (0, j))],
        out_specs=pl.BlockSpec((tm, tn), lambda i, j: (i, j)),
        out_shape=jax.ShapeDtypeStruct((s, n), BF16),
        scratch_shapes=[pltpu.VMEM((tm, d), BF16)],
        compiler_params=_params(2),
        name="in_proj",
    )(x, g, w)


def _swa_body(sinks_ref, q_ref, kv_ref, kvp_ref, pos_ref, posp_ref, invf_ref, gq_ref,
              gk_ref, bd_ref, o_ref, klo_ref, khi_ref, vlo_ref, vhi_ref, bias_ref, *, tq):
    i = pl.program_id(0)
    nblk = tq // ATTN_BLOCK
    lane = lax.broadcasted_iota(jnp.int32, (1, LANES), 1)
    first_half = (lane % HEAD_DIM) < (HEAD_DIM // 2)
    low_head = lane < HEAD_DIM
    invf = invf_ref[...]
    bd = bd_ref[...]

    def rope_tables(pos):
        ang = pos * invf
        sin = jnp.sin(ang)
        return jnp.cos(ang), jnp.where(first_half, -sin, sin)

    def norm_rope(xf, g, cos, sin_signed):
        ms = jnp.dot((xf * xf).astype(BF16), bd, preferred_element_type=F32)
        xn = xf * lax.rsqrt(ms + EPS) * g
        swapped = jnp.where(first_half, pltpu.roll(xn, LANES - HEAD_DIM // 2, 1),
                            pltpu.roll(xn, HEAD_DIM // 2, 1))
        return xn * cos + swapped * sin_signed

    cos_c, sin_c = rope_tables(pos_ref[...])
    cos_p, sin_p = rope_tables(posp_ref[...])
    gk = gk_ref[...]

    def stage_kv(kv_tile, cos, sin_signed, row0, rows):
        for kc in range(N_KV_HEADS // 2):
            kf = kv_tile[:, kc * LANES:(kc + 1) * LANES].astype(F32)
            kr = norm_rope(kf, gk, cos, sin_signed)
            vf = kv_tile[:, (2 + kc) * LANES:(3 + kc) * LANES].astype(F32)
            for src, lo_ref, hi_ref in ((kr, klo_ref, khi_ref), (vf, vlo_ref, vhi_ref)):
                rolled = pltpu.roll(src, HEAD_DIM, 1)
                zero = jnp.zeros_like(src)
                sl = pl.ds(row0, rows)
                lo_ref[2 * kc, sl, :] = jnp.where(low_head, src, zero).astype(BF16)
                hi_ref[2 * kc, sl, :] = jnp.where(low_head, zero, rolled).astype(BF16)
                lo_ref[2 * kc + 1, sl, :] = jnp.where(low_head, rolled, zero).astype(BF16)
                hi_ref[2 * kc + 1, sl, :] = jnp.where(low_head, zero, src).astype(BF16)

    stage_kv(kvp_ref[...], cos_p, sin_p, 0, ATTN_BLOCK)
    stage_kv(kv_ref[...], cos_c, sin_c, ATTN_BLOCK, tq)

    qi = lax.broadcasted_iota(jnp.int32, (tq, 2 * ATTN_BLOCK), 0)
    kj = lax.broadcasted_iota(jnp.int32, (tq, 2 * ATTN_BLOCK), 1)
    diff = qi % ATTN_BLOCK + ATTN_BLOCK - kj
    first_key = jnp.where(i > 0, 0, ATTN_BLOCK)
    visible = ((diff >= 0) & (diff < ATTN_BLOCK)
               & ((kj >= first_key) | (qi >= ATTN_BLOCK)))
    bias_ref[...] = jnp.where(visible, 0.0, NEG)
    gq = gq_ref[...]
    ones = jnp.ones((2 * ATTN_BLOCK, LANES), BF16)
    nt = (((1,), (1,)), ((), ()))
    blocks = [(slice(n * ATTN_BLOCK, (n + 1) * ATTN_BLOCK),
               pl.ds(n * ATTN_BLOCK, 2 * ATTN_BLOCK)) for n in range(nblk)]

    for pc in range(N_HEADS // 2):
        g = pc // 2
        qf = q_ref[:, pc * LANES:(pc + 1) * LANES].astype(F32)
        qr = (norm_rope(qf, gq, cos_c, sin_c) * (HEAD_DIM ** -0.5)).astype(BF16)
        out = None
        for hh, (k_ref, v_ref) in enumerate(((klo_ref, vlo_ref), (khi_ref, vhi_ref))):
            s = jnp.concatenate(
                [lax.dot_general(qr[rows], k_ref[g, keys, :], nt, preferred_element_type=F32)
                 for rows, keys in blocks], axis=0) + bias_ref[...]
            sink = sinks_ref[2 * pc + hh]
            m = jnp.maximum(jnp.max(s, axis=-1, keepdims=True), sink)
            e = jnp.exp(s - m).astype(BF16)
            den = jnp.dot(e, ones, preferred_element_type=F32) + jnp.exp(sink - m)
            o = jnp.concatenate(
                [jnp.dot(e[rows], v_ref[g, keys, :], preferred_element_type=F32)
                 for rows, keys in blocks], axis=0) / den
            out = o if out is None else out + o
        o_ref[:, pc * LANES:(pc + 1) * LANES] = out.astype(o_ref.dtype)


def _swa(proj, pos_b, invf, gq, gk, bd, sinks, tq):
    s = proj.shape[0]
    rpb = tq // ATTN_BLOCK
    kv_blk = (N_HEADS * HEAD_DIM) // (2 * N_KV_HEADS * HEAD_DIM)
    prev = lambda i: jnp.maximum(i * rpb - 1, 0)
    rows = tq + ATTN_BLOCK
    stage = pltpu.VMEM((N_KV_HEADS, rows, LANES), BF16)
    return pl.pallas_call(
        functools.partial(_swa_body, tq=tq),
        grid=(s // tq,),
        in_specs=[pl.BlockSpec(memory_space=pltpu.SMEM),
                  pl.BlockSpec((tq, N_HEADS * HEAD_DIM), lambda i: (i, 0)),
                  pl.BlockSpec((tq, 2 * N_KV_HEADS * HEAD_DIM), lambda i: (i, kv_blk)),
                  pl.BlockSpec((ATTN_BLOCK, 2 * N_KV_HEADS * HEAD_DIM),
                               lambda i: (prev(i), kv_blk)),
                  pl.BlockSpec((tq, LANES), lambda i: (i, 0)),
                  pl.BlockSpec((ATTN_BLOCK, LANES), lambda i: (prev(i), 0)),
                  pl.BlockSpec((1, LANES), lambda i: (0, 0)),
                  pl.BlockSpec((1, LANES), lambda i: (0, 0)),
                  pl.BlockSpec((1, LANES), lambda i: (0, 0)),
                  pl.BlockSpec((LANES, LANES), lambda i: (0, 0))],
        out_specs=pl.BlockSpec((tq, N_HEADS * HEAD_DIM), lambda i: (i, 0)),
        out_shape=jax.ShapeDtypeStruct((s, N_HEADS * HEAD_DIM), BF16),
        scratch_shapes=[stage, stage, stage, stage,
                        pltpu.VMEM((tq, 2 * ATTN_BLOCK), F32)],
        compiler_params=_params(1),
        name="swa",
    )(sinks, proj, proj, proj, pos_b, pos_b, invf, gq, gk, bd)


def _conv_body(a_ref, b_ref, ah_ref, bh_ref, w_ref, b0_ref, lg_ref, lb_ref, o_ref,
               ext_ref, c_ref, *, tm):
    i = pl.program_id(0)
    ext_ref[CONV_HALO:, :] = a_ref[...].astype(F32) * jax.nn.sigmoid(b_ref[...].astype(F32))
    halo = ah_ref[...].astype(F32) * jax.nn.sigmoid(bh_ref[...].astype(F32))
    ext_ref[:CONV_HALO, :] = jnp.where(i > 0, halo, jnp.zeros_like(halo))
    ch = ext_ref.shape[1]
    rows = 128
    for cc in range(ch // LANES):
        cols = slice(cc * LANES, (cc + 1) * LANES)
        for rc in range(tm // rows):
            acc = jnp.broadcast_to(b0_ref[:, cols], (rows, LANES))
            for w in range(CONV_WIDTH):
                start = rc * rows + CONV_HALO - (CONV_WIDTH - 1) + w
                acc = acc + ext_ref[start:start + rows, cols] * w_ref[w:w + 1, cols]
            c_ref[rc * rows:(rc + 1) * rows, cols] = acc
    c = c_ref[...]
    mu = jnp.mean(c, axis=-1, keepdims=True)
    xc = c - mu
    y = xc * lax.rsqrt(jnp.mean(xc * xc, axis=-1, keepdims=True) + EPS)
    y = y * lg_ref[...] + lb_ref[...]
    o_ref[...] = (y * jax.nn.sigmoid(y)).astype(o_ref.dtype)


def _conv(proj, dw_w, dw_b, ln_g, ln_b, tm, a_col, ch):
    s = proj.shape[0]
    a_blk = a_col // ch
    hpb = tm // CONV_HALO
    prev = lambda i: jnp.maximum(i * hpb - 1, 0)
    vec = pl.BlockSpec((1, ch), lambda i: (0, 0))
    return pl.pallas_call(
        functools.partial(_conv_body, tm=tm),
        grid=(s // tm,),
        in_specs=[pl.BlockSpec((tm, ch), lambda i: (i, a_blk)),
                  pl.BlockSpec((tm, ch), lambda i: (i, a_blk + 1)),
                  pl.BlockSpec((CONV_HALO, ch), lambda i: (prev(i), a_blk)),
                  pl.BlockSpec((CONV_HALO, ch), lambda i: (prev(i), a_blk + 1)),
                  pl.BlockSpec((CONV_WIDTH, ch), lambda i: (0, 0)),
                  vec, vec, vec],
        out_specs=pl.BlockSpec((tm, ch), lambda i: (i, 0)),
        out_shape=jax.ShapeDtypeStruct((s, ch), BF16),
        scratch_shapes=[pltpu.VMEM((tm + CONV_HALO, ch), F32), pltpu.VMEM((tm, ch), F32)],
        compiler_params=_params(1),
        name="conv",
    )(proj, proj, proj, proj, dw_w, dw_b, ln_g, ln_b)


def _mem_kv_body(mem_ref, g_ref, w_ref, kg_ref, k_ref, v_ref):
    m = mem_ref[...]
    ms = jnp.mean(m * m, axis=-1, keepdims=True)
    h = (m * lax.rsqrt(ms + EPS) * g_ref[...]).astype(BF16)
    kv = jnp.dot(h, w_ref[...], preferred_element_type=F32)
    width = MEM_HEADS * MEM_HEAD_DIM
    for hd in range(MEM_HEADS):
        cols = slice(hd * MEM_HEAD_DIM, (hd + 1) * MEM_HEAD_DIM)
        k = kv[:, cols]
        kms = jnp.mean(k * k, axis=-1, keepdims=True)
        k_ref[:, cols] = (k * lax.rsqrt(kms + EPS) * kg_ref[...]).astype(BF16)
    v_ref[...] = kv[:, width:].astype(BF16)


def _mem_kv(mem, g, w, kg):
    m = mem.shape[0]
    width = MEM_HEADS * MEM_HEAD_DIM
    out = jax.ShapeDtypeStruct((m, width), BF16)
    return pl.pallas_call(
        _mem_kv_body,
        out_shape=(out, out),
        compiler_params=pltpu.CompilerParams(vmem_limit_bytes=VMEM_LIMIT),
        name="mem_kv",
    )(mem, g, w, kg)


def _mem_attn_body(q_ref, k_ref, v_ref, qg_ref, o_ref):
    nt = (((1,), (1,)), ((), ()))
    for hd in range(MEM_HEADS):
        cols = slice(hd * MEM_HEAD_DIM, (hd + 1) * MEM_HEAD_DIM)
        q = q_ref[:, cols].astype(F32)
        qms = jnp.mean(q * q, axis=-1, keepdims=True)
        qn = (q * lax.rsqrt(qms + EPS) * qg_ref[...]).astype(BF16)
        s = lax.dot_general(qn, k_ref[:, cols], nt, preferred_element_type=F32)
        s = s * (MEM_HEAD_DIM ** -0.5)
        e = jnp.exp(s - jnp.max(s, axis=-1, keepdims=True))
        p = (e / jnp.sum(e, axis=-1, keepdims=True)).astype(BF16)
        o_ref[:, cols] = jnp.dot(p, v_ref[:, cols],
                                 preferred_element_type=F32).astype(o_ref.dtype)


def _mem_attn(proj, mk, mv, qg, tm, q_col):
    s = proj.shape[0]
    width = MEM_HEADS * MEM_HEAD_DIM
    m = mk.shape[0]
    return pl.pallas_call(
        _mem_attn_body,
        grid=(s // tm,),
        in_specs=[pl.BlockSpec((tm, width), lambda i: (i, q_col // width)),
                  pl.BlockSpec((m, width), lambda i: (0, 0)),
                  pl.BlockSpec((m, width), lambda i: (0, 0)),
                  pl.BlockSpec((1, MEM_HEAD_DIM), lambda i: (0, 0))],
        out_specs=pl.BlockSpec((tm, width), lambda i: (i, 0)),
        out_shape=jax.ShapeDtypeStruct((s, width), BF16),
        compiler_params=_params(1),
        name="mem_attn",
    )(proj, mk, mv, qg)


def _mix_out_body(x_ref, at_ref, cv_ref, mm_ref, ga0, ga1, gc0, gc1, gm0, gm1,
                  wa_ref, wc_ref, wm_ref, wo_ref, o_ref):
    at = at_ref[...]
    cv = cv_ref[...]
    mm = mm_ref[...]
    acc = x_ref[...]
    half = wo_ref.shape[0] // 2
    for c, (ga, gc, gm) in enumerate(((ga0, gc0, gm0), (ga1, gc1, gm1))):
        cols = slice(c * half, (c + 1) * half)
        merged = jax.nn.sigmoid(ga[...].astype(F32)) * jnp.dot(
            at, wa_ref[:, cols], preferred_element_type=F32)
        merged += jax.nn.sigmoid(gc[...].astype(F32)) * jnp.dot(
            cv, wc_ref[:, cols], preferred_element_type=F32)
        merged += jax.nn.sigmoid(gm[...].astype(F32)) * jnp.dot(
            mm, wm_ref[:, cols], preferred_element_type=F32)
        acc = acc + jnp.dot(merged.astype(BF16), wo_ref[cols, :], preferred_element_type=F32)
    o_ref[...] = acc


def _mix_out(x, attn, conv, memo, proj, wa, wc, wm, wo, tm, gate_col):
    s, d = x.shape
    half = d // 2
    g0 = gate_col // half
    gate = lambda k: pl.BlockSpec((tm, half), lambda i: (i, g0 + k))
    full = lambda a: pl.BlockSpec(a.shape, lambda i: (0, 0))
    return pl.pallas_call(
        _mix_out_body,
        grid=(s // tm,),
        in_specs=[pl.BlockSpec((tm, d), lambda i: (i, 0)),
                  pl.BlockSpec((tm, attn.shape[1]), lambda i: (i, 0)),
                  pl.BlockSpec((tm, conv.shape[1]), lambda i: (i, 0)),
                  pl.BlockSpec((tm, memo.shape[1]), lambda i: (i, 0)),
                  gate(0), gate(1), gate(2), gate(3), gate(4), gate(5),
                  full(wa), full(wc), full(wm), full(wo)],
        out_specs=pl.BlockSpec((tm, d), lambda i: (i, 0)),
        out_shape=jax.ShapeDtypeStruct((s, d), F32),
        compiler_params=_params(1),
        name="mix_out",
    )(x, attn, conv, memo, proj, proj, proj, proj, proj, proj, wa, wc, wm, wo)


def _rank_code(r):
    return -(RANK_BASE + r * RANK_STEP)


def _top16(s):
    vals = []
    for r in range(PEER_TOPK):
        m = jnp.max(s, axis=0, keepdims=True)
        s = jnp.where(s == m, _rank_code(r), s)
        vals.append(m)
    return vals, s


def _pair_counts(v1, v2):
    shape = (8, v1[0].shape[1])
    row = lax.broadcasted_iota(jnp.int32, shape, 0)
    rowf = row.astype(F32)
    v2lo = jnp.concatenate(v2[:8], axis=0)
    v2hi = jnp.concatenate(v2[8:], axis=0)
    ninf = jnp.full(shape, -jnp.inf, F32)
    groups = [v1[0] + v2lo, v1[0] + v2hi, v1[1] + v2lo]
    index = [rowf, rowf + 8.0, rowf + 16.0]
    for r1 in range(2, 8):
        groups.append(jnp.where(row < PEER_TOPK // (r1 + 1), v1[r1] + v2lo, ninf))
        index.append(rowf + 16.0 * r1)
    groups.append(jnp.concatenate(v1[8:], axis=0) + v2[0])
    index.append(16.0 * (rowf + 8.0))
    cand = jnp.concatenate(groups, axis=0)
    idx = jnp.concatenate(index, axis=0)
    taken = jnp.zeros(cand.shape, F32)
    best = v1[0] + v2[0]
    z = jnp.zeros_like(best)
    for _ in range(PEER_TOPK):
        m = jnp.max(cand, axis=0, keepdims=True)
        first = jnp.min(jnp.where(cand == m, idx, 4096.0), axis=0, keepdims=True)
        sel = idx == first
        cand = jnp.where(sel, -jnp.inf, cand)
        taken = jnp.where(sel, 1.0, taken)
        z = z + jnp.exp(m - best)
    cnt = [jnp.sum(taken[0:16], axis=0, keepdims=True)]
    for r1 in range(1, 8):
        cnt.append(jnp.sum(taken[8 * (r1 + 1):8 * (r1 + 2)], axis=0, keepdims=True))
    cnt += [taken[72 + r:73 + r] for r in range(8)]
    return cnt, z


def _peer_route_body(x_ref, g_ref, wq_ref, keys_ref, ht_ref, cnt_ref, e1_ref, rk_ref, e2_ref,
                     q_ref, *, tf):
    x = x_ref[...]
    ms = jnp.mean(x * x, axis=-1, keepdims=True)
    ht = (x * lax.rsqrt(ms + EPS) * g_ref[...]).T.astype(BF16)
    ht_ref[...] = ht
    q_ref[...] = jnp.dot(wq_ref[...], ht, preferred_element_type=F32).astype(BF16)

    def head(hd, carry):
        q1 = q_ref[pl.ds(pl.multiple_of(hd * 2 * N_KEYS, 2 * N_KEYS), N_KEYS), :]
        q2 = q_ref[pl.ds(pl.multiple_of(hd * 2 * N_KEYS + N_KEYS, N_KEYS), N_KEYS), :]
        s1 = jnp.dot(keys_ref[2 * hd], q1, preferred_element_type=F32)
        s2 = jnp.dot(keys_ref[2 * hd + 1], q2, preferred_element_type=F32)
        for c in range(tf // LANES):
            cols = slice(c * LANES, (c + 1) * LANES)
            s1c = s1[:, cols]
            s2c = s2[:, cols]
            v1, coded1 = _top16(s1c)
            v2, coded2 = _top16(s2c)
            cnt, z = _pair_counts(v1, v2)
            cnt_i = jnp.zeros_like(coded1)
            for r in range(PEER_TOPK):
                cnt_i = jnp.where(coded1 == _rank_code(r), cnt[r], cnt_i)
            rank2 = jnp.where(coded2 < -0.5 * RANK_BASE,
                              (-coded2 - RANK_BASE) * (1.0 / RANK_STEP), UNRANKED)
            cnt_ref[hd, :, cols] = cnt_i
            e1_ref[hd, :, cols] = jnp.exp(s1c - v1[0]) / z
            rk_ref[hd, :, cols] = rank2.astype(BF16)
            e2_ref[hd, :, cols] = jnp.exp(s2c - v2[0]).astype(BF16)
        return carry

    lax.fori_loop(0, PEER_HEADS, head, 0)


def _peer_route(x1, g, wq_t, keys, tf):
    s, d = x1.shape
    qw = wq_t.shape[0]
    tab = lambda dt: jax.ShapeDtypeStruct((PEER_HEADS, N_KEYS, s), dt)
    tab_spec = pl.BlockSpec((PEER_HEADS, N_KEYS, tf), lambda i: (0, 0, i))
    return pl.pallas_call(
        functools.partial(_peer_route_body, tf=tf),
        grid=(s // tf,),
        in_specs=[pl.BlockSpec((tf, d), lambda i: (i, 0)),
                  pl.BlockSpec((1, d), lambda i: (0, 0)),
                  pl.BlockSpec((qw, d), lambda i: (0, 0)),
                  pl.BlockSpec(keys.shape, lambda i: (0, 0, 0))],
        out_specs=(pl.BlockSpec((d, tf), lambda i: (0, i)),
                   tab_spec, tab_spec, tab_spec, tab_spec),
        out_shape=(jax.ShapeDtypeStruct((d, s), BF16), tab(F32), tab(F32), tab(BF16), tab(BF16)),
        scratch_shapes=[pltpu.VMEM((qw, tf), BF16)],
        compiler_params=_params(1),
        name="peer_route",
    )(x1, g, wq_t, keys)


def _peer_experts_body(ht_ref, u_ref, vt_ref, cnt_ref, e1_ref, rk_ref, e2_ref, x_ref, o_ref,
                       acc_ref, *, rows_per_chunk, n_parts):
    c = pl.program_id(1)
    tm = ht_ref.shape[1]
    rows_per_part = rows_per_chunk // n_parts

    @pl.when(c == 0)
    def _():
        acc_ref[...] = jnp.zeros_like(acc_ref)

    ht = ht_ref[...]
    total = None
    for p in range(n_parts):
        first = p * rows_per_part
        part = slice(first * N_KEYS, (first + rows_per_part) * N_KEYS)
        a = jnp.dot(u_ref[part, :], ht, preferred_element_type=F32)
        acts = []
        for r in range(first, first + rows_per_part):
            ar = a[(r - first) * N_KEYS:(r - first + 1) * N_KEYS]
            gel = (0.5 * ar * (1.0 + lax.erf(ar * (2.0 ** -0.5)))).astype(BF16)
            gate = None
            for hd in range(PEER_HEADS):
                cnt = jnp.broadcast_to(cnt_ref[hd, r:r + 1, :], (N_KEYS, tm)).astype(BF16)
                e1 = jnp.broadcast_to(e1_ref[hd, r:r + 1, :], (N_KEYS, tm)).astype(BF16)
                e2 = e2_ref[hd]
                term = jnp.where(rk_ref[hd] < cnt, e2, jnp.zeros_like(e2)) * e1
                gate = term if gate is None else gate + term
            acts.append(gel * gate)
        act = jnp.concatenate(acts, axis=0)
        out = jnp.dot(vt_ref[:, part], act, preferred_element_type=F32)
        total = out if total is None else total + out

    acc_ref[...] += total

    @pl.when(c == pl.num_programs(1) - 1)
    def _():
        o_ref[...] = x_ref[...] + acc_ref[...].T


def _peer_experts(ht, u, vt, cnt, e1, rk, e2, x1, tm, ec, n_parts):
    s, d = x1.shape
    n_exp = u.shape[0]
    rpc = ec // N_KEYS
    tab = pl.BlockSpec((PEER_HEADS, N_KEYS, tm), lambda i, c: (0, 0, i))
    row = pl.BlockSpec((PEER_HEADS, rpc, tm), lambda i, c: (0, c, i))
    return pl.pallas_call(
        functools.partial(_peer_experts_body, rows_per_chunk=rpc, n_parts=n_parts),
        grid=(s // tm, n_exp // ec),
        in_specs=[pl.BlockSpec((d, tm), lambda i, c: (0, i)),
                  pl.BlockSpec((ec, d), lambda i, c: (c, 0)),
                  pl.BlockSpec((d, ec), lambda i, c: (0, c)),
                  row, row, tab, tab,
                  pl.BlockSpec((tm, d), lambda i, c: (i, 0))],
        out_specs=pl.BlockSpec((tm, d), lambda i, c: (i, 0)),
        out_shape=jax.ShapeDtypeStruct((s, d), F32),
        scratch_shapes=[pltpu.VMEM((d, tm), F32)],
        compiler_params=_params(2),
        name="peer_experts",
    )(ht, u, vt, cnt, e1, rk, e2, x1)


def _tile(s, want):
    return min(s, want)


def kernel(x, mem, positions, g_mix, w_in, q_norm_g, k_norm_g, attn_sinks, w_attn_o,
           conv_dw_w, conv_dw_b, conv_ln_g, conv_ln_b, w_conv_o, g_mem, w_mem_kv,
           mq_norm_g, mk_norm_g, w_mem_o, w_out, g_ffn, w_query, sub_keys, expert_u, expert_v):
    b, s, d = x.shape
    depth = g_mix.shape[0]
    q_w = N_HEADS * HEAD_DIM
    kv_w = N_KV_HEADS * HEAD_DIM
    conv_ch = conv_dw_b.shape[-1]
    mem_w = MEM_HEADS * MEM_HEAD_DIM
    conv_col = q_w + 2 * kv_w
    mq_col = conv_col + 2 * conv_ch
    gate_col = mq_col + mem_w

    row = lambda v: v.reshape(1, -1).astype(F32)
    inv_freq = ROPE_THETA ** (-jnp.arange(0, HEAD_DIM, 2, dtype=F32) / HEAD_DIM)
    invf = jnp.tile(inv_freq, LANES // (HEAD_DIM // 2)).reshape(1, LANES)
    head_of_lane = jnp.arange(LANES) // HEAD_DIM
    bd = (head_of_lane[:, None] == head_of_lane[None, :]).astype(BF16) * (1.0 / HEAD_DIM)

    outs = []
    for bi in range(b):
        xb = x[bi]
        pos_b = jnp.broadcast_to(positions[bi].astype(F32)[:, None], (s, LANES))
        for l in range(depth):
            proj = _in_proj(xb, row(g_mix[l]), w_in[l].astype(BF16), _tile(s, 1024), 1024)
            attn = _swa(proj, pos_b, invf, jnp.tile(row(q_norm_g[l]), (1, 2)),
                        jnp.tile(row(k_norm_g[l]), (1, 2)), bd, attn_sinks[l].astype(F32),
                        _tile(s, 512))
            conv = _conv(proj, conv_dw_w[l].reshape(CONV_WIDTH, conv_ch), row(conv_dw_b[l]),
                         row(conv_ln_g[l]), row(conv_ln_b[l]), _tile(s, 512), conv_col, conv_ch)
            mk, mv = _mem_kv(mem[bi], row(g_mem[l]), w_mem_kv[l].astype(BF16),
                             row(mk_norm_g[l]))
            memo = _mem_attn(proj, mk, mv, row(mq_norm_g[l]), _tile(s, 512), mq_col)
            x1 = _mix_out(xb, attn, conv, memo, proj, w_attn_o[l].astype(BF16),
                          w_conv_o[l].astype(BF16), w_mem_o[l].astype(BF16),
                          w_out[l].astype(BF16), _tile(s, 512), gate_col)
            keys = sub_keys[l].reshape(2 * PEER_HEADS, N_KEYS, -1).astype(BF16)
            h2, cnt, e1, rk, e2 = _peer_route(x1, row(g_ffn[l]), w_query[l].T.astype(BF16),
                                              keys, _tile(s, 256))
            xb = _peer_experts(h2, expert_u[l].astype(BF16), expert_v[l].T.astype(BF16),
                               cnt, e1, rk, e2, x1, _tile(s, 512), 1024, 2)
        outs.append(xb)
    return jnp.stack(outs)
```

```python
import functools
from typing import NamedTuple

import jax
import jax.numpy as jnp
from jax import lax
from jax.experimental import pallas as pl
from jax.experimental.pallas import tpu as pltpu

F32 = jnp.float32
BF16 = jnp.bfloat16

EPS = 1e-6
NEG = -1e30
LANES = 128
SUBLANES = 8
HEAD_DIM = 64
N_HEADS = 16
N_KV_HEADS = 4
ATTN_BLOCK = 128
ROPE_THETA = 10000.0
CONV_WIDTH = 31
CONV_HALO = 32
MEM_HEADS = 4
MEM_HEAD_DIM = 128
PEER_HEADS = 8
N_KEYS = 128
PEER_TOPK = 16
UNRANKED = 64.0
RANK_BASE = 2.0 ** 100
RANK_STEP = 2.0 ** 77

VMEM_LIMIT = 56 * 1024 * 1024


def _params(n_axes, vmem=VMEM_LIMIT, **kwargs):
    return pltpu.CompilerParams(
        dimension_semantics=("arbitrary",) * n_axes, vmem_limit_bytes=vmem, **kwargs)


def _in_proj_body(x_ref, g_ref, w_ref, o_ref, h_ref):
    @pl.when(pl.program_id(1) == 0)
    def _():
        x = x_ref[...]
        ms = jnp.mean(x * x, axis=-1, keepdims=True)
        h_ref[...] = (x * lax.rsqrt(ms + EPS) * g_ref[...]).astype(BF16)

    o_ref[...] = jnp.dot(h_ref[...], w_ref[...],
                         preferred_element_type=F32).astype(o_ref.dtype)


def _in_proj(x, g, w, tm, tn):
    s, d = x.shape
    n = w.shape[1]
    return pl.pallas_call(
        _in_proj_body,
        grid=(s // tm, n // tn),
        in_specs=[pl.BlockSpec((tm, d), lambda i, j: (i, 0)),
                  pl.BlockSpec((1, d), lambda i, j: (0, 0)),
                  pl.BlockSpec((d, tn), lambda i, j: (0, j))],
        out_specs=pl.BlockSpec((tm, tn), lambda i, j: (i, j)),
        out_shape=jax.ShapeDtypeStruct((s, n), BF16),
        scratch_shapes=[pltpu.VMEM((tm, d), BF16)],
        compiler_params=_params(2),
        name="in_proj",
    )(x, g, w)


def _swa_body(sinks_ref, q_ref, kv_ref, kvp_ref, pos_ref, posp_ref, invf_ref, gq_ref,
              gk_ref, bd_ref, o_ref, klo_ref, khi_ref, vlo_ref, vhi_ref, bias_ref, *, tq):
    i = pl.program_id(0)
    nblk = tq // ATTN_BLOCK
    lane = lax.broadcasted_iota(jnp.int32, (1, LANES), 1)
    first_half = (lane % HEAD_DIM) < (HEAD_DIM // 2)
    low_head = lane < HEAD_DIM
    invf = invf_ref[...]
    bd = bd_ref[...]

    def rope_tables(pos):
        ang = pos * invf
        sin = jnp.sin(ang)
        return jnp.cos(ang), jnp.where(first_half, -sin, sin)

    def norm_rope(xf, g, cos, sin_signed):
        ms = jnp.dot((xf * xf).astype(BF16), bd, preferred_element_type=F32)
        xn = xf * lax.rsqrt(ms + EPS) * g
        swapped = jnp.where(first_half, pltpu.roll(xn, LANES - HEAD_DIM // 2, 1),
                            pltpu.roll(xn, HEAD_DIM // 2, 1))
        return xn * cos + swapped * sin_signed

    cos_c, sin_c = rope_tables(pos_ref[...])
    cos_p, sin_p = rope_tables(posp_ref[...])
    gk = gk_ref[...]

    def stage_kv(kv_tile, cos, sin_signed, row0, rows):
        for kc in range(N_KV_HEADS // 2):
            kf = kv_tile[:, kc * LANES:(kc + 1) * LANES].astype(F32)
            kr = norm_rope(kf, gk, cos, sin_signed)
            vf = kv_tile[:, (2 + kc) * LANES:(3 + kc) * LANES].astype(F32)
            for src, lo_ref, hi_ref in ((kr, klo_ref, khi_ref), (vf, vlo_ref, vhi_ref)):
                rolled = pltpu.roll(src, HEAD_DIM, 1)
                zero = jnp.zeros_like(src)
                sl = pl.ds(row0, rows)
                lo_ref[2 * kc, sl, :] = jnp.where(low_head, src, zero).astype(BF16)
                hi_ref[2 * kc, sl, :] = jnp.where(low_head, zero, rolled).astype(BF16)
                lo_ref[2 * kc + 1, sl, :] = jnp.where(low_head, rolled, zero).astype(BF16)
                hi_ref[2 * kc + 1, sl, :] = jnp.where(low_head, zero, src).astype(BF16)

    stage_kv(kvp_ref[...], cos_p, sin_p, 0, ATTN_BLOCK)
    stage_kv(kv_ref[...], cos_c, sin_c, ATTN_BLOCK, tq)

    qi = lax.broadcasted_iota(jnp.int32, (tq, 2 * ATTN_BLOCK), 0)
    kj = lax.broadcasted_iota(jnp.int32, (tq, 2 * ATTN_BLOCK), 1)
    diff = qi % ATTN_BLOCK + ATTN_BLOCK - kj
    first_key = jnp.where(i > 0, 0, ATTN_BLOCK)
    visible = ((diff >= 0) & (diff < ATTN_BLOCK)
               & ((kj >= first_key) | (qi >= ATTN_BLOCK)))
    bias_ref[...] = jnp.where(visible, 0.0, NEG)
    gq = gq_ref[...]
    ones = jnp.ones((2 * ATTN_BLOCK, LANES), BF16)
    nt = (((1,), (1,)), ((), ()))
    blocks = [(slice(n * ATTN_BLOCK, (n + 1) * ATTN_BLOCK),
               pl.ds(n * ATTN_BLOCK, 2 * ATTN_BLOCK)) for n in range(nblk)]

    for pc in range(N_HEADS // 2):
        g = pc // 2
        qf = q_ref[:, pc * LANES:(pc + 1) * LANES].astype(F32)
        qr = (norm_rope(qf, gq, cos_c, sin_c) * (HEAD_DIM ** -0.5)).astype(BF16)
        out = None
        for hh, (k_ref, v_ref) in enumerate(((klo_ref, vlo_ref), (khi_ref, vhi_ref))):
            s = jnp.concatenate(
                [lax.dot_general(qr[rows], k_ref[g, keys, :], nt, preferred_element_type=F32)
                 for rows, keys in blocks], axis=0) + bias_ref[...]
            sink = sinks_ref[2 * pc + hh]
            m = jnp.maximum(jnp.max(s, axis=-1, keepdims=True), sink)
            e = jnp.exp(s - m).astype(BF16)
            den = jnp.dot(e, ones, preferred_element_type=F32) + jnp.exp(sink - m)
            o = jnp.concatenate(
                [jnp.dot(e[rows], v_ref[g, keys, :], preferred_element_type=F32)
                 for rows, keys in blocks], axis=0) / den
            out = o if out is None else out + o
        o_ref[:, pc * LANES:(pc + 1) * LANES] = out.astype(o_ref.dtype)


def _swa(proj, pos_b, invf, gq, gk, bd, sinks, tq):
    s = proj.shape[0]
    rpb = tq // ATTN_BLOCK
    kv_blk = (N_HEADS * HEAD_DIM) // (2 * N_KV_HEADS * HEAD_DIM)
    prev = lambda i: jnp.maximum(i * rpb - 1, 0)
    rows = tq + ATTN_BLOCK
    stage = pltpu.VMEM((N_KV_HEADS, rows, LANES), BF16)
    return pl.pallas_call(
        functools.partial(_swa_body, tq=tq),
        grid=(s // tq,),
        in_specs=[pl.BlockSpec(memory_space=pltpu.SMEM),
                  pl.BlockSpec((tq, N_HEADS * HEAD_DIM), lambda i: (i, 0)),
                  pl.BlockSpec((tq, 2 * N_KV_HEADS * HEAD_DIM), lambda i: (i, kv_blk)),
                  pl.BlockSpec((ATTN_BLOCK, 2 * N_KV_HEADS * HEAD_DIM),
                               lambda i: (prev(i), kv_blk)),
                  pl.BlockSpec((tq, LANES), lambda i: (i, 0)),
                  pl.BlockSpec((ATTN_BLOCK, LANES), lambda i: (prev(i), 0)),
                  pl.BlockSpec((1, LANES), lambda i: (0, 0)),
                  pl.BlockSpec((1, LANES), lambda i: (0, 0)),
                  pl.BlockSpec((1, LANES), lambda i: (0, 0)),
                  pl.BlockSpec((LANES, LANES), lambda i: (0, 0))],
        out_specs=pl.BlockSpec((tq, N_HEADS * HEAD_DIM), lambda i: (i, 0)),
        out_shape=jax.ShapeDtypeStruct((s, N_HEADS * HEAD_DIM), BF16),
        scratch_shapes=[stage, stage, stage, stage,
                        pltpu.VMEM((tq, 2 * ATTN_BLOCK), F32)],
        compiler_params=_params(1),
        name="swa",
    )(sinks, proj, proj, proj, pos_b, pos_b, invf, gq, gk, bd)


def _conv_body(a_ref, b_ref, ah_ref, bh_ref, w_ref, b0_ref, lg_ref, lb_ref, o_ref,
               ext_ref, shift_ref, c_ref, *, tm):
    i = pl.program_id(0)
    ext_ref[CONV_HALO:, :] = a_ref[...].astype(F32) * jax.nn.sigmoid(b_ref[...].astype(F32))
    halo = ah_ref[...].astype(F32) * jax.nn.sigmoid(bh_ref[...].astype(F32))
    ext_ref[:CONV_HALO, :] = jnp.where(i > 0, halo, jnp.zeros_like(halo))
    span = shift_ref.shape[1]
    for k in range(1, SUBLANES):
        shift_ref[k] = ext_ref[k:k + span, :]
    ch = ext_ref.shape[1]
    rows = 128
    for cc in range(ch // LANES):
        cols = slice(cc * LANES, (cc + 1) * LANES)
        for rc in range(tm // rows):
            acc = jnp.broadcast_to(b0_ref[:, cols], (rows, LANES))
            for w in range(CONV_WIDTH):
                start = rc * rows + CONV_HALO - (CONV_WIDTH - 1) + w
                k = start % SUBLANES
                if k == 0:
                    window = ext_ref[start:start + rows, cols]
                else:
                    window = shift_ref[k, start - k:start - k + rows, cols]
                acc = acc + window * w_ref[w:w + 1, cols]
            c_ref[rc * rows:(rc + 1) * rows, cols] = acc
    c = c_ref[...]
    mu = jnp.mean(c, axis=-1, keepdims=True)
    xc = c - mu
    y = xc * lax.rsqrt(jnp.mean(xc * xc, axis=-1, keepdims=True) + EPS)
    y = y * lg_ref[...] + lb_ref[...]
    o_ref[...] = (y * jax.nn.sigmoid(y)).astype(o_ref.dtype)


def _conv(proj, dw_w, dw_b, ln_g, ln_b, tm, a_col, ch):
    s = proj.shape[0]
    a_blk = a_col // ch
    hpb = tm // CONV_HALO
    prev = lambda i: jnp.maximum(i * hpb - 1, 0)
    vec = pl.BlockSpec((1, ch), lambda i: (0, 0))
    return pl.pallas_call(
        functools.partial(_conv_body, tm=tm),
        grid=(s // tm,),
        in_specs=[pl.BlockSpec((tm, ch), lambda i: (i, a_blk)),
                  pl.BlockSpec((tm, ch), lambda i: (i, a_blk + 1)),
                  pl.BlockSpec((CONV_HALO, ch), lambda i: (prev(i), a_blk)),
                  pl.BlockSpec((CONV_HALO, ch), lambda i: (prev(i), a_blk + 1)),
                  pl.BlockSpec((CONV_WIDTH, ch), lambda i: (0, 0)),
                  vec, vec, vec],
        out_specs=pl.BlockSpec((tm, ch), lambda i: (i, 0)),
        out_shape=jax.ShapeDtypeStruct((s, ch), BF16),
        scratch_shapes=[pltpu.VMEM((tm + CONV_HALO, ch), F32),
                        pltpu.VMEM((SUBLANES, tm + CONV_HALO - SUBLANES, ch), F32),
                        pltpu.VMEM((tm, ch), F32)],
        compiler_params=_params(1),
        name="conv",
    )(proj, proj, proj, proj, dw_w, dw_b, ln_g, ln_b)


def _mem_kv_body(mem_ref, g_ref, w_ref, kg_ref, k_ref, v_ref):
    m = mem_ref[...]
    ms = jnp.mean(m * m, axis=-1, keepdims=True)
    h = (m * lax.rsqrt(ms + EPS) * g_ref[...]).astype(BF16)
    kv = jnp.dot(h, w_ref[...], preferred_element_type=F32)
    width = MEM_HEADS * MEM_HEAD_DIM
    for hd in range(MEM_HEADS):
        cols = slice(hd * MEM_HEAD_DIM, (hd + 1) * MEM_HEAD_DIM)
        k = kv[:, cols]
        kms = jnp.mean(k * k, axis=-1, keepdims=True)
        k_ref[:, cols] = (k * lax.rsqrt(kms + EPS) * kg_ref[...]).astype(BF16)
    v_ref[...] = kv[:, width:].astype(BF16)


def _mem_kv(mem, g, w, kg):
    m = mem.shape[0]
    width = MEM_HEADS * MEM_HEAD_DIM
    out = jax.ShapeDtypeStruct((m, width), BF16)
    return pl.pallas_call(
        _mem_kv_body,
        out_shape=(out, out),
        compiler_params=pltpu.CompilerParams(vmem_limit_bytes=VMEM_LIMIT),
        name="mem_kv",
    )(mem, g, w, kg)


def _mem_attn_body(q_ref, k_ref, v_ref, qg_ref, o_ref):
    nt = (((1,), (1,)), ((), ()))
    for hd in range(MEM_HEADS):
        cols = slice(hd * MEM_HEAD_DIM, (hd + 1) * MEM_HEAD_DIM)
        q = q_ref[:, cols].astype(F32)
        qms = jnp.mean(q * q, axis=-1, keepdims=True)
        qn = (q * lax.rsqrt(qms + EPS) * qg_ref[...]).astype(BF16)
        s = lax.dot_general(qn, k_ref[:, cols], nt, preferred_element_type=F32)
        s = s * (MEM_HEAD_DIM ** -0.5)
        e = jnp.exp(s - jnp.max(s, axis=-1, keepdims=True))
        p = (e / jnp.sum(e, axis=-1, keepdims=True)).astype(BF16)
        o_ref[:, cols] = jnp.dot(p, v_ref[:, cols],
                                 preferred_element_type=F32).astype(o_ref.dtype)


def _mem_attn(proj, mk, mv, qg, tm, q_col):
    s = proj.shape[0]
    width = MEM_HEADS * MEM_HEAD_DIM
    m = mk.shape[0]
    return pl.pallas_call(
        _mem_attn_body,
        grid=(s // tm,),
        in_specs=[pl.BlockSpec((tm, width), lambda i: (i, q_col // width)),
                  pl.BlockSpec((m, width), lambda i: (0, 0)),
                  pl.BlockSpec((m, width), lambda i: (0, 0)),
                  pl.BlockSpec((1, MEM_HEAD_DIM), lambda i: (0, 0))],
        out_specs=pl.BlockSpec((tm, width), lambda i: (i, 0)),
        out_shape=jax.ShapeDtypeStruct((s, width), BF16),
        compiler_params=_params(1),
        name="mem_attn",
    )(proj, mk, mv, qg)


def _mix_out_body(x_ref, at_ref, cv_ref, mm_ref, ga0, ga1, gc0, gc1, gm0, gm1,
                  wa_ref, wc_ref, wm_ref, wo_ref, o_ref):
    at = at_ref[...]
    cv = cv_ref[...]
    mm = mm_ref[...]
    acc = x_ref[...]
    half = wo_ref.shape[0] // 2
    for c, (ga, gc, gm) in enumerate(((ga0, gc0, gm0), (ga1, gc1, gm1))):
        cols = slice(c * half, (c + 1) * half)
        merged = jax.nn.sigmoid(ga[...].astype(F32)) * jnp.dot(
            at, wa_ref[:, cols], preferred_element_type=F32)
        merged += jax.nn.sigmoid(gc[...].astype(F32)) * jnp.dot(
            cv, wc_ref[:, cols], preferred_element_type=F32)
        merged += jax.nn.sigmoid(gm[...].astype(F32)) * jnp.dot(
            mm, wm_ref[:, cols], preferred_element_type=F32)
        acc = acc + jnp.dot(merged.astype(BF16), wo_ref[cols, :], preferred_element_type=F32)
    o_ref[...] = acc


def _mix_out(x, attn, conv, memo, proj, wa, wc, wm, wo, tm, gate_col):
    s, d = x.shape
    half = d // 2
    g0 = gate_col // half
    gate = lambda k: pl.BlockSpec((tm, half), lambda i: (i, g0 + k))
    full = lambda a: pl.BlockSpec(a.shape, lambda i: (0, 0))
    return pl.pallas_call(
        _mix_out_body,
        grid=(s // tm,),
        in_specs=[pl.BlockSpec((tm, d), lambda i: (i, 0)),
                  pl.BlockSpec((tm, attn.shape[1]), lambda i: (i, 0)),
                  pl.BlockSpec((tm, conv.shape[1]), lambda i: (i, 0)),
                  pl.BlockSpec((tm, memo.shape[1]), lambda i: (i, 0)),
                  gate(0), gate(1), gate(2), gate(3), gate(4), gate(5),
                  full(wa), full(wc), full(wm), full(wo)],
        out_specs=pl.BlockSpec((tm, d), lambda i: (i, 0)),
        out_shape=jax.ShapeDtypeStruct((s, d), F32),
        compiler_params=_params(1),
        name="mix_out",
    )(x, attn, conv, memo, proj, proj, proj, proj, proj, proj, wa, wc, wm, wo)


def _rank_code(r):
    return -(RANK_BASE + r * RANK_STEP)


def _top16(s):
    vals = []
    for r in range(PEER_TOPK):
        m = jnp.max(s, axis=0, keepdims=True)
        s = jnp.where(s == m, _rank_code(r), s)
        vals.append(m)
    return vals, s


def _pair_counts(v1, v2):
    shape = (8, v1[0].shape[1])
    row = lax.broadcasted_iota(jnp.int32, shape, 0)
    rowf = row.astype(F32)
    v2lo = jnp.concatenate(v2[:8], axis=0)
    v2hi = jnp.concatenate(v2[8:], axis=0)
    ninf = jnp.full(shape, -jnp.inf, F32)
    groups = [v1[0] + v2lo, v1[0] + v2hi, v1[1] + v2lo]
    index = [rowf, rowf + 8.0, rowf + 16.0]
    for r1 in range(2, 8):
        groups.append(jnp.where(row < PEER_TOPK // (r1 + 1), v1[r1] + v2lo, ninf))
        index.append(rowf + 16.0 * r1)
    groups.append(jnp.concatenate(v1[8:], axis=0) + v2[0])
    index.append(16.0 * (rowf + 8.0))
    cand = jnp.concatenate(groups, axis=0)
    idx = jnp.concatenate(index, axis=0)
    remaining = cand
    best = v1[0] + v2[0]
    z = jnp.zeros_like(best)
    for _ in range(PEER_TOPK):
        m = jnp.max(remaining, axis=0, keepdims=True)
        first = jnp.min(jnp.where(remaining == m, idx, 4096.0), axis=0, keepdims=True)
        remaining = jnp.where(idx == first, -jnp.inf, remaining)
        z = z + jnp.exp(m - best)
    taken = jnp.where(remaining != cand, 1.0, 0.0)
    cnt = [jnp.sum(taken[0:16], axis=0, keepdims=True)]
    for r1 in range(1, 8):
        cnt.append(jnp.sum(taken[8 * (r1 + 1):8 * (r1 + 2)], axis=0, keepdims=True))
    cnt += [taken[72 + r:73 + r] for r in range(8)]
    return cnt, z


def _peer_route_body(x_ref, g_ref, wq_ref, keys_ref, ht_ref, cnt_ref, e1_ref, rk_ref, e2_ref,
                     q_ref, *, tf):
    x = x_ref[...]
    ms = jnp.mean(x * x, axis=-1, keepdims=True)
    ht = (x * lax.rsqrt(ms + EPS) * g_ref[...]).T.astype(BF16)
    ht_ref[...] = ht
    q_ref[...] = jnp.dot(wq_ref[...], ht, preferred_element_type=F32).astype(BF16)

    def head(hd, carry):
        q1 = q_ref[pl.ds(pl.multiple_of(hd * 2 * N_KEYS, 2 * N_KEYS), N_KEYS), :]
        q2 = q_ref[pl.ds(pl.multiple_of(hd * 2 * N_KEYS + N_KEYS, N_KEYS), N_KEYS), :]
        s1 = jnp.dot(keys_ref[2 * hd], q1, preferred_element_type=F32)
        s2 = jnp.dot(keys_ref[2 * hd + 1], q2, preferred_element_type=F32)
        for c in range(tf // LANES):
            cols = slice(c * LANES, (c + 1) * LANES)
            s1c = s1[:, cols]
            s2c = s2[:, cols]
            v1, coded1 = _top16(s1c)
            v2, coded2 = _top16(s2c)
            cnt, z = _pair_counts(v1, v2)
            cnt_i = jnp.zeros_like(coded1)
            for r in range(PEER_TOPK):
                cnt_i = jnp.where(coded1 == _rank_code(r), cnt[r], cnt_i)
            rank2 = jnp.where(coded2 < -0.5 * RANK_BASE,
                              (-coded2 - RANK_BASE) * (1.0 / RANK_STEP), UNRANKED)
            cnt_ref[hd, :, cols] = cnt_i
            e1_ref[hd, :, cols] = jnp.exp(s1c - v1[0]) / z
            rk_ref[hd, :, cols] = rank2.astype(BF16)
            e2_ref[hd, :, cols] = jnp.exp(s2c - v2[0]).astype(BF16)
        return carry

    lax.fori_loop(0, PEER_HEADS, head, 0, unroll=True)


def _peer_route(x1, g, wq_t, keys, tf):
    s, d = x1.shape
    qw = wq_t.shape[0]
    tab = lambda dt: jax.ShapeDtypeStruct((PEER_HEADS, N_KEYS, s), dt)
    tab_spec = pl.BlockSpec((PEER_HEADS, N_KEYS, tf), lambda i: (0, 0, i))
    return pl.pallas_call(
        functools.partial(_peer_route_body, tf=tf),
        grid=(s // tf,),
        in_specs=[pl.BlockSpec((tf, d), lambda i: (i, 0)),
                  pl.BlockSpec((1, d), lambda i: (0, 0)),
                  pl.BlockSpec((qw, d), lambda i: (0, 0)),
                  pl.BlockSpec(keys.shape, lambda i: (0, 0, 0))],
        out_specs=(pl.BlockSpec((d, tf), lambda i: (0, i)),
                   tab_spec, tab_spec, tab_spec, tab_spec),
        out_shape=(jax.ShapeDtypeStruct((d, s), BF16), tab(F32), tab(F32), tab(BF16), tab(BF16)),
        scratch_shapes=[pltpu.VMEM((qw, tf), BF16)],
        compiler_params=_params(1),
        name="peer_route",
    )(x1, g, wq_t, keys)


def _peer_experts_body(ht_ref, u_ref, vt_ref, cnt_ref, e1_ref, rk_ref, e2_ref, x_ref, o_ref,
                       acc_ref, *, rows_per_chunk, n_parts):
    c = pl.program_id(1)
    tm = ht_ref.shape[1]
    rows_per_part = rows_per_chunk // n_parts

    @pl.when(c == 0)
    def _():
        acc_ref[...] = jnp.zeros_like(acc_ref)

    ht = ht_ref[...]

    def part_rows(p):
        return slice(p * rows_per_part * N_KEYS, (p + 1) * rows_per_part * N_KEYS)

    def project(p):
        return jnp.dot(u_ref[part_rows(p), :], ht, preferred_element_type=F32)

    def activate(p, a):
        acts = []
        for k in range(rows_per_part):
            r = p * rows_per_part + k
            ar = a[k * N_KEYS:(k + 1) * N_KEYS]
            gel = (0.5 * ar * (1.0 + lax.erf(ar * (2.0 ** -0.5)))).astype(BF16)
            gate = None
            for hd in range(PEER_HEADS):
                cnt = jnp.broadcast_to(cnt_ref[hd, r:r + 1, :], (N_KEYS, tm)).astype(BF16)
                e1 = jnp.broadcast_to(e1_ref[hd, r:r + 1, :], (N_KEYS, tm)).astype(BF16)
                e2 = e2_ref[hd]
                term = jnp.where(rk_ref[hd] < cnt, e2, jnp.zeros_like(e2)) * e1
                gate = term if gate is None else gate + term
            acts.append(gel * gate)
        return jnp.concatenate(acts, axis=0)

    act = jnp.concatenate([activate(p, project(p)) for p in range(n_parts)], axis=0)
    acc_ref[...] += jnp.dot(vt_ref[...], act, preferred_element_type=F32)

    @pl.when(c == pl.num_programs(1) - 1)
    def _():
        o_ref[...] = x_ref[...] + acc_ref[...].T


def _peer_experts(ht, u, vt, cnt, e1, rk, e2, x1, tm, ec, n_parts):
    s, d = x1.shape
    n_exp = u.shape[0]
    rpc = ec // N_KEYS
    tab = pl.BlockSpec((PEER_HEADS, N_KEYS, tm), lambda i, c: (0, 0, i))
    row = pl.BlockSpec((PEER_HEADS, rpc, tm), lambda i, c: (0, c, i))
    return pl.pallas_call(
        functools.partial(_peer_experts_body, rows_per_chunk=rpc, n_parts=n_parts),
        grid=(s // tm, n_exp // ec),
        in_specs=[pl.BlockSpec((d, tm), lambda i, c: (0, i)),
                  pl.BlockSpec((ec, d), lambda i, c: (c, 0)),
                  pl.BlockSpec((d, ec), lambda i, c: (0, c)),
                  row, row, tab, tab,
                  pl.BlockSpec((tm, d), lambda i, c: (i, 0))],
        out_specs=pl.BlockSpec((tm, d), lambda i, c: (i, 0)),
        out_shape=jax.ShapeDtypeStruct((s, d), F32),
        scratch_shapes=[pltpu.VMEM((d, tm), F32)],
        compiler_params=_params(2),
        name="peer_experts",
    )(ht, u, vt, cnt, e1, rk, e2, x1)


class _Tiles(NamedTuple):
    in_proj_m: int
    in_proj_n: int
    mixer: int
    route: int
    experts_m: int
    experts_chunk: int
    experts_parts: int


def _tiles(s):
    return _Tiles(in_proj_m=min(s, 1024), in_proj_n=1536, mixer=min(s, 512),
                  route=min(s, 256), experts_m=min(s, 512), experts_chunk=1024,
                  experts_parts=8)


def kernel(x, mem, positions, g_mix, w_in, q_norm_g, k_norm_g, attn_sinks, w_attn_o,
           conv_dw_w, conv_dw_b, conv_ln_g, conv_ln_b, w_conv_o, g_mem, w_mem_kv,
           mq_norm_g, mk_norm_g, w_mem_o, w_out, g_ffn, w_query, sub_keys, expert_u, expert_v):
    b, s, d = x.shape
    depth = g_mix.shape[0]
    q_w = N_HEADS * HEAD_DIM
    kv_w = N_KV_HEADS * HEAD_DIM
    conv_ch = conv_dw_b.shape[-1]
    mem_w = MEM_HEADS * MEM_HEAD_DIM
    conv_col = q_w + 2 * kv_w
    mq_col = conv_col + 2 * conv_ch
    gate_col = mq_col + mem_w
    t = _tiles(s)

    row = lambda v: v.reshape(1, -1).astype(F32)
    inv_freq = ROPE_THETA ** (-jnp.arange(0, HEAD_DIM, 2, dtype=F32) / HEAD_DIM)
    invf = jnp.tile(inv_freq, LANES // (HEAD_DIM // 2)).reshape(1, LANES)
    head_of_lane = jnp.arange(LANES) // HEAD_DIM
    bd = (head_of_lane[:, None] == head_of_lane[None, :]).astype(BF16) * (1.0 / HEAD_DIM)

    outs = []
    for bi in range(b):
        xb = x[bi]
        pos_b = jnp.broadcast_to(positions[bi].astype(F32)[:, None], (s, LANES))
        for l in range(depth):
            proj = _in_proj(xb, row(g_mix[l]), w_in[l].astype(BF16), t.in_proj_m, t.in_proj_n)
            attn = _swa(proj, pos_b, invf, jnp.tile(row(q_norm_g[l]), (1, 2)),
                        jnp.tile(row(k_norm_g[l]), (1, 2)), bd, attn_sinks[l].astype(F32),
                        t.mixer)
            conv = _conv(proj, conv_dw_w[l].reshape(CONV_WIDTH, conv_ch), row(conv_dw_b[l]),
                         row(conv_ln_g[l]), row(conv_ln_b[l]), t.mixer, conv_col, conv_ch)
            mk, mv = _mem_kv(mem[bi], row(g_mem[l]), w_mem_kv[l].astype(BF16),
                             row(mk_norm_g[l]))
            memo = _mem_attn(proj, mk, mv, row(mq_norm_g[l]), t.mixer, mq_col)
            x1 = _mix_out(xb, attn, conv, memo, proj, w_attn_o[l].astype(BF16),
                          w_conv_o[l].astype(BF16), w_mem_o[l].astype(BF16),
                          w_out[l].astype(BF16), t.mixer, gate_col)
            keys = sub_keys[l].reshape(2 * PEER_HEADS, N_KEYS, -1).astype(BF16)
            ht, cnt, e1, rk, e2 = _peer_route(x1, row(g_ffn[l]), w_query[l].astype(BF16).T,
                                              keys, t.route)
            xb = _peer_experts(ht, expert_u[l].astype(BF16), expert_v[l].astype(BF16).T,
                               cnt, e1, rk, e2, x1, t.experts_m, t.experts_chunk,
                               t.experts_parts)
        outs.append(xb)
    return jnp.stack(outs)
```

```python
import functools
from typing import NamedTuple

import jax
import jax.numpy as jnp
from jax import lax
from jax.experimental import pallas as pl
from jax.experimental.pallas import tpu as pltpu

F32 = jnp.float32
BF16 = jnp.bfloat16

EPS = 1e-6
NEG = -1e30
LANES = 128
SUBLANES = 8
HEAD_DIM = 64
N_HEADS = 16
N_KV_HEADS = 4
ATTN_BLOCK = 128
ROPE_THETA = 10000.0
CONV_WIDTH = 31
CONV_HALO = 32
MEM_HEADS = 4
MEM_HEAD_DIM = 128
PEER_HEADS = 8
N_KEYS = 128
PEER_TOPK = 16
UNRANKED = 64.0
RANK_BASE = 2.0 ** 100
RANK_STEP = 2.0 ** 77

VMEM_LIMIT = 56 * 1024 * 1024


def _params(n_axes, vmem=VMEM_LIMIT, **kwargs):
    return pltpu.CompilerParams(
        dimension_semantics=("arbitrary",) * n_axes, vmem_limit_bytes=vmem, **kwargs)


def _in_proj_body(x_ref, g_ref, w_ref, o_ref, h_ref):
    @pl.when(pl.program_id(1) == 0)
    def _():
        x = x_ref[...]
        ms = jnp.mean(x * x, axis=-1, keepdims=True)
        h_ref[...] = (x * lax.rsqrt(ms + EPS) * g_ref[...]).astype(BF16)

    o_ref[...] = jnp.dot(h_ref[...], w_ref[...],
                         preferred_element_type=F32).astype(o_ref.dtype)


def _in_proj(x, g, w, tm, tn):
    s, d = x.shape
    n = w.shape[1]
    return pl.pallas_call(
        _in_proj_body,
        grid=(s // tm, n // tn),
        in_specs=[pl.BlockSpec((tm, d), lambda i, j: (i, 0)),
                  pl.BlockSpec((1, d), lambda i, j: (0, 0)),
                  pl.BlockSpec((d, tn), lambda i, j: (0, j))],
        out_specs=pl.BlockSpec((tm, tn), lambda i, j: (i, j)),
        out_shape=jax.ShapeDtypeStruct((s, n), BF16),
        scratch_shapes=[pltpu.VMEM((tm, d), BF16)],
        compiler_params=_params(2),
        name="in_proj",
    )(x, g, w)


def _swa_body(sinks_ref, q_ref, kv_ref, kvp_ref, pos_ref, posp_ref, invf_ref, gq_ref,
              gk_ref, bd_ref, o_ref, klo_ref, khi_ref, vlo_ref, vhi_ref, bias_ref, *, tq):
    i = pl.program_id(0)
    nblk = tq // ATTN_BLOCK
    lane = lax.broadcasted_iota(jnp.int32, (1, LANES), 1)
    first_half = (lane % HEAD_DIM) < (HEAD_DIM // 2)
    low_head = lane < HEAD_DIM
    invf = invf_ref[...]
    bd = bd_ref[...]

    def rope_tables(pos):
        ang = pos * invf
        sin = jnp.sin(ang)
        return jnp.cos(ang), jnp.where(first_half, -sin, sin)

    def norm_rope(xf, g, cos, sin_signed):
        ms = jnp.dot((xf * xf).astype(BF16), bd, preferred_element_type=F32)
        xn = xf * lax.rsqrt(ms + EPS) * g
        swapped = jnp.where(first_half, pltpu.roll(xn, LANES - HEAD_DIM // 2, 1),
                            pltpu.roll(xn, HEAD_DIM // 2, 1))
        return xn * cos + swapped * sin_signed

    cos_c, sin_c = rope_tables(pos_ref[...])
    cos_p, sin_p = rope_tables(posp_ref[...])
    gk = gk_ref[...]

    def stage_kv(kv_tile, cos, sin_signed, row0, rows):
        for kc in range(N_KV_HEADS // 2):
            kf = kv_tile[:, kc * LANES:(kc + 1) * LANES].astype(F32)
            kr = norm_rope(kf, gk, cos, sin_signed)
            vf = kv_tile[:, (2 + kc) * LANES:(3 + kc) * LANES].astype(F32)
            for src, lo_ref, hi_ref in ((kr, klo_ref, khi_ref), (vf, vlo_ref, vhi_ref)):
                rolled = pltpu.roll(src, HEAD_DIM, 1)
                zero = jnp.zeros_like(src)
                sl = pl.ds(row0, rows)
                lo_ref[2 * kc, sl, :] = jnp.where(low_head, src, zero).astype(BF16)
                hi_ref[2 * kc, sl, :] = jnp.where(low_head, zero, rolled).astype(BF16)
                lo_ref[2 * kc + 1, sl, :] = jnp.where(low_head, rolled, zero).astype(BF16)
                hi_ref[2 * kc + 1, sl, :] = jnp.where(low_head, zero, src).astype(BF16)

    stage_kv(kvp_ref[...], cos_p, sin_p, 0, ATTN_BLOCK)
    stage_kv(kv_ref[...], cos_c, sin_c, ATTN_BLOCK, tq)

    qi = lax.broadcasted_iota(jnp.int32, (tq, 2 * ATTN_BLOCK), 0)
    kj = lax.broadcasted_iota(jnp.int32, (tq, 2 * ATTN_BLOCK), 1)
    diff = qi % ATTN_BLOCK + ATTN_BLOCK - kj
    first_key = jnp.where(i > 0, 0, ATTN_BLOCK)
    visible = ((diff >= 0) & (diff < ATTN_BLOCK)
               & ((kj >= first_key) | (qi >= ATTN_BLOCK)))
    bias_ref[...] = jnp.where(visible, 0.0, NEG)
    gq = gq_ref[...]
    ones = jnp.ones((2 * ATTN_BLOCK, LANES), BF16)
    nt = (((1,), (1,)), ((), ()))
    blocks = [(slice(n * ATTN_BLOCK, (n + 1) * ATTN_BLOCK),
               pl.ds(n * ATTN_BLOCK, 2 * ATTN_BLOCK)) for n in range(nblk)]

    for pc in range(N_HEADS // 2):
        g = pc // 2
        qf = q_ref[:, pc * LANES:(pc + 1) * LANES].astype(F32)
        qr = (norm_rope(qf, gq, cos_c, sin_c) * (HEAD_DIM ** -0.5)).astype(BF16)
        out = None
        for hh, (k_ref, v_ref) in enumerate(((klo_ref, vlo_ref), (khi_ref, vhi_ref))):
            s = jnp.concatenate(
                [lax.dot_general(qr[rows], k_ref[g, keys, :], nt, preferred_element_type=F32)
                 for rows, keys in blocks], axis=0) + bias_ref[...]
            sink = sinks_ref[2 * pc + hh]
            m = jnp.maximum(jnp.max(s, axis=-1, keepdims=True), sink)
            e = jnp.exp(s - m).astype(BF16)
            den = jnp.dot(e, ones, preferred_element_type=F32) + jnp.exp(sink - m)
            o = jnp.concatenate(
                [jnp.dot(e[rows], v_ref[g, keys, :], preferred_element_type=F32)
                 for rows, keys in blocks], axis=0) / den
            out = o if out is None else out + o
        o_ref[:, pc * LANES:(pc + 1) * LANES] = out.astype(o_ref.dtype)


def _swa(proj, pos_b, invf, gq, gk, bd, sinks, tq):
    s = proj.shape[0]
    rpb = tq // ATTN_BLOCK
    kv_blk = (N_HEADS * HEAD_DIM) // (2 * N_KV_HEADS * HEAD_DIM)
    prev = lambda i: jnp.maximum(i * rpb - 1, 0)
    rows = tq + ATTN_BLOCK
    stage = pltpu.VMEM((N_KV_HEADS, rows, LANES), BF16)
    return pl.pallas_call(
        functools.partial(_swa_body, tq=tq),
        grid=(s // tq,),
        in_specs=[pl.BlockSpec(memory_space=pltpu.SMEM),
                  pl.BlockSpec((tq, N_HEADS * HEAD_DIM), lambda i: (i, 0)),
                  pl.BlockSpec((tq, 2 * N_KV_HEADS * HEAD_DIM), lambda i: (i, kv_blk)),
                  pl.BlockSpec((ATTN_BLOCK, 2 * N_KV_HEADS * HEAD_DIM),
                               lambda i: (prev(i), kv_blk)),
                  pl.BlockSpec((tq, LANES), lambda i: (i, 0)),
                  pl.BlockSpec((ATTN_BLOCK, LANES), lambda i: (prev(i), 0)),
                  pl.BlockSpec((1, LANES), lambda i: (0, 0)),
                  pl.BlockSpec((1, LANES), lambda i: (0, 0)),
                  pl.BlockSpec((1, LANES), lambda i: (0, 0)),
                  pl.BlockSpec((LANES, LANES), lambda i: (0, 0))],
        out_specs=pl.BlockSpec((tq, N_HEADS * HEAD_DIM), lambda i: (i, 0)),
        out_shape=jax.ShapeDtypeStruct((s, N_HEADS * HEAD_DIM), BF16),
        scratch_shapes=[stage, stage, stage, stage,
                        pltpu.VMEM((tq, 2 * ATTN_BLOCK), F32)],
        compiler_params=_params(1),
        name="swa",
    )(sinks, proj, proj, proj, pos_b, pos_b, invf, gq, gk, bd)


def _conv_body(a_ref, b_ref, ah_ref, bh_ref, w_ref, b0_ref, lg_ref, lb_ref, o_ref,
               ext_ref, shift_ref, c_ref, *, tm):
    i = pl.program_id(0)
    ext_ref[CONV_HALO:, :] = a_ref[...].astype(F32) * jax.nn.sigmoid(b_ref[...].astype(F32))
    halo = ah_ref[...].astype(F32) * jax.nn.sigmoid(bh_ref[...].astype(F32))
    ext_ref[:CONV_HALO, :] = jnp.where(i > 0, halo, jnp.zeros_like(halo))
    span = shift_ref.shape[1]
    for k in range(1, SUBLANES):
        shift_ref[k] = ext_ref[k:k + span, :]
    ch = ext_ref.shape[1]
    rows = 128
    for cc in range(ch // LANES):
        cols = slice(cc * LANES, (cc + 1) * LANES)
        for rc in range(tm // rows):
            acc = jnp.broadcast_to(b0_ref[:, cols], (rows, LANES))
            for w in range(CONV_WIDTH):
                start = rc * rows + CONV_HALO - (CONV_WIDTH - 1) + w
                k = start % SUBLANES
                if k == 0:
                    window = ext_ref[start:start + rows, cols]
                else:
                    window = shift_ref[k, start - k:start - k + rows, cols]
                acc = acc + window * w_ref[w:w + 1, cols]
            c_ref[rc * rows:(rc + 1) * rows, cols] = acc
    c = c_ref[...]
    mu = jnp.mean(c, axis=-1, keepdims=True)
    xc = c - mu
    y = xc * lax.rsqrt(jnp.mean(xc * xc, axis=-1, keepdims=True) + EPS)
    y = y * lg_ref[...] + lb_ref[...]
    o_ref[...] = (y * jax.nn.sigmoid(y)).astype(o_ref.dtype)


def _conv(proj, dw_w, dw_b, ln_g, ln_b, tm, a_col, ch):
    s = proj.shape[0]
    a_blk = a_col // ch
    hpb = tm // CONV_HALO
    prev = lambda i: jnp.maximum(i * hpb - 1, 0)
    vec = pl.BlockSpec((1, ch), lambda i: (0, 0))
    return pl.pallas_call(
        functools.partial(_conv_body, tm=tm),
        grid=(s // tm,),
        in_specs=[pl.BlockSpec((tm, ch), lambda i: (i, a_blk)),
                  pl.BlockSpec((tm, ch), lambda i: (i, a_blk + 1)),
                  pl.BlockSpec((CONV_HALO, ch), lambda i: (prev(i), a_blk)),
                  pl.BlockSpec((CONV_HALO, ch), lambda i: (prev(i), a_blk + 1)),
                  pl.BlockSpec((CONV_WIDTH, ch), lambda i: (0, 0)),
                  vec, vec, vec],
        out_specs=pl.BlockSpec((tm, ch), lambda i: (i, 0)),
        out_shape=jax.ShapeDtypeStruct((s, ch), BF16),
        scratch_shapes=[pltpu.VMEM((tm + CONV_HALO, ch), F32),
                        pltpu.VMEM((SUBLANES, tm + CONV_HALO - SUBLANES, ch), F32),
                        pltpu.VMEM((tm, ch), F32)],
        compiler_params=_params(1),
        name="conv",
    )(proj, proj, proj, proj, dw_w, dw_b, ln_g, ln_b)


def _mem_kv_body(mem_ref, g_ref, w_ref, kg_ref, k_ref, v_ref):
    m = mem_ref[...]
    ms = jnp.mean(m * m, axis=-1, keepdims=True)
    h = (m * lax.rsqrt(ms + EPS) * g_ref[...]).astype(BF16)
    kv = jnp.dot(h, w_ref[...], preferred_element_type=F32)
    width = MEM_HEADS * MEM_HEAD_DIM
    for hd in range(MEM_HEADS):
        cols = slice(hd * MEM_HEAD_DIM, (hd + 1) * MEM_HEAD_DIM)
        k = kv[:, cols]
        kms = jnp.mean(k * k, axis=-1, keepdims=True)
        k_ref[:, cols] = (k * lax.rsqrt(kms + EPS) * kg_ref[...]).astype(BF16)
    v_ref[...] = kv[:, width:].astype(BF16)


def _mem_kv(mem, g, w, kg):
    m = mem.shape[0]
    width = MEM_HEADS * MEM_HEAD_DIM
    out = jax.ShapeDtypeStruct((m, width), BF16)
    return pl.pallas_call(
        _mem_kv_body,
        out_shape=(out, out),
        compiler_params=pltpu.CompilerParams(vmem_limit_bytes=VMEM_LIMIT),
        name="mem_kv",
    )(mem, g, w, kg)


def _mem_attn_body(q_ref, k_ref, v_ref, qg_ref, o_ref):
    nt = (((1,), (1,)), ((), ()))
    for hd in range(MEM_HEADS):
        cols = slice(hd * MEM_HEAD_DIM, (hd + 1) * MEM_HEAD_DIM)
        q = q_ref[:, cols].astype(F32)
        qms = jnp.mean(q * q, axis=-1, keepdims=True)
        qn = (q * lax.rsqrt(qms + EPS) * qg_ref[...]).astype(BF16)
        s = lax.dot_general(qn, k_ref[:, cols], nt, preferred_element_type=F32)
        s = s * (MEM_HEAD_DIM ** -0.5)
        e = jnp.exp(s - jnp.max(s, axis=-1, keepdims=True))
        p = (e / jnp.sum(e, axis=-1, keepdims=True)).astype(BF16)
        o_ref[:, cols] = jnp.dot(p, v_ref[:, cols],
                                 preferred_element_type=F32).astype(o_ref.dtype)


def _mem_attn(proj, mk, mv, qg, tm, q_col):
    s = proj.shape[0]
    width = MEM_HEADS * MEM_HEAD_DIM
    m = mk.shape[0]
    return pl.pallas_call(
        _mem_attn_body,
        grid=(s // tm,),
        in_specs=[pl.BlockSpec((tm, width), lambda i: (i, q_col // width)),
                  pl.BlockSpec((m, width), lambda i: (0, 0)),
                  pl.BlockSpec((m, width), lambda i: (0, 0)),
                  pl.BlockSpec((1, MEM_HEAD_DIM), lambda i: (0, 0))],
        out_specs=pl.BlockSpec((tm, width), lambda i: (i, 0)),
        out_shape=jax.ShapeDtypeStruct((s, width), BF16),
        compiler_params=_params(1),
        name="mem_attn",
    )(proj, mk, mv, qg)


def _mix_out_body(x_ref, at_ref, cv_ref, mm_ref, ga0, ga1, gc0, gc1, gm0, gm1,
                  wa_ref, wc_ref, wm_ref, wo_ref, o_ref):
    at = at_ref[...]
    cv = cv_ref[...]
    mm = mm_ref[...]
    acc = x_ref[...]
    half = wo_ref.shape[0] // 2
    for c, (ga, gc, gm) in enumerate(((ga0, gc0, gm0), (ga1, gc1, gm1))):
        cols = slice(c * half, (c + 1) * half)
        merged = jax.nn.sigmoid(ga[...].astype(F32)) * jnp.dot(
            at, wa_ref[:, cols], preferred_element_type=F32)
        merged += jax.nn.sigmoid(gc[...].astype(F32)) * jnp.dot(
            cv, wc_ref[:, cols], preferred_element_type=F32)
        merged += jax.nn.sigmoid(gm[...].astype(F32)) * jnp.dot(
            mm, wm_ref[:, cols], preferred_element_type=F32)
        acc = acc + jnp.dot(merged.astype(BF16), wo_ref[cols, :], preferred_element_type=F32)
    o_ref[...] = acc


def _mix_out(x, attn, conv, memo, proj, wa, wc, wm, wo, tm, gate_col):
    s, d = x.shape
    half = d // 2
    g0 = gate_col // half
    gate = lambda k: pl.BlockSpec((tm, half), lambda i: (i, g0 + k))
    full = lambda a: pl.BlockSpec(a.shape, lambda i: (0, 0))
    return pl.pallas_call(
        _mix_out_body,
        grid=(s // tm,),
        in_specs=[pl.BlockSpec((tm, d), lambda i: (i, 0)),
                  pl.BlockSpec((tm, attn.shape[1]), lambda i: (i, 0)),
                  pl.BlockSpec((tm, conv.shape[1]), lambda i: (i, 0)),
                  pl.BlockSpec((tm, memo.shape[1]), lambda i: (i, 0)),
                  gate(0), gate(1), gate(2), gate(3), gate(4), gate(5),
                  full(wa), full(wc), full(wm), full(wo)],
        out_specs=pl.BlockSpec((tm, d), lambda i: (i, 0)),
        out_shape=jax.ShapeDtypeStruct((s, d), F32),
        compiler_params=_params(1),
        name="mix_out",
    )(x, attn, conv, memo, proj, proj, proj, proj, proj, proj, wa, wc, wm, wo)


def _rank_code(r):
    return -(RANK_BASE + r * RANK_STEP)


def _top16(s):
    vals = []
    for r in range(PEER_TOPK):
        m = jnp.max(s, axis=0, keepdims=True)
        s = jnp.where(s == m, _rank_code(r), s)
        vals.append(m)
    return vals, s


def _pair_counts(v1, v2):
    shape = (SUBLANES, v1[0].shape[1])
    row = lax.broadcasted_iota(jnp.int32, shape, 0)
    rowf = row.astype(F32)
    v2lo = jnp.concatenate(v2[:8], axis=0)
    v2hi = jnp.concatenate(v2[8:], axis=0)
    ninf = jnp.full(shape, -jnp.inf, F32)

    def shifted(k):
        return pltpu.roll(v2lo, k, 0)

    groups = [v1[0] + v2lo, v1[0] + v2hi, v1[1] + v2lo,
              jnp.where(row < 5, v1[2] + v2lo, v1[4] + shifted(5)),
              jnp.where(row < 4, v1[3] + v2lo,
                        jnp.where(row < 6, v1[5] + shifted(4), v1[6] + shifted(6))),
              jnp.where(row < 2, v1[7] + v2lo, ninf),
              jnp.concatenate(v1[8:], axis=0) + v2[0]]
    index = [rowf, rowf + 8.0, rowf + 16.0,
             jnp.where(row < 5, rowf + 32.0, rowf + (64.0 - 5.0)),
             jnp.where(row < 4, rowf + 48.0,
                       jnp.where(row < 6, rowf + (80.0 - 4.0), rowf + (96.0 - 6.0))),
             rowf + 112.0,
             16.0 * (rowf + 8.0)]
    cand = jnp.concatenate(groups, axis=0)
    idx = jnp.concatenate(index, axis=0)
    remaining = cand
    best = v1[0] + v2[0]
    z = jnp.zeros_like(best)
    for _ in range(PEER_TOPK):
        m = jnp.max(remaining, axis=0, keepdims=True)
        first = jnp.min(jnp.where(remaining == m, idx, 4096.0), axis=0, keepdims=True)
        remaining = jnp.where(idx == first, -jnp.inf, remaining)
        z = z + jnp.exp(m - best)
    taken = jnp.where(remaining != cand, 1.0, 0.0)

    def count(group, lo, hi):
        rows = taken[SUBLANES * group:SUBLANES * (group + 1)]
        return jnp.sum(jnp.where((row >= lo) & (row < hi), rows, 0.0), axis=0, keepdims=True)

    cnt = [jnp.sum(taken[0:16], axis=0, keepdims=True), count(2, 0, 8),
           count(3, 0, 5), count(4, 0, 4), count(3, 5, 8), count(4, 4, 6), count(4, 6, 8),
           count(5, 0, 2)]
    cnt += [taken[48 + r:49 + r] for r in range(8)]
    return cnt, z


def _peer_route_body(x_ref, g_ref, wq_ref, keys_ref, ht_ref, cnt_ref, e1_ref, rk_ref, e2_ref,
                     q_ref, *, tf):
    x = x_ref[...]
    ms = jnp.mean(x * x, axis=-1, keepdims=True)
    ht = (x * lax.rsqrt(ms + EPS) * g_ref[...]).T.astype(BF16)
    ht_ref[...] = ht
    q_ref[...] = jnp.dot(wq_ref[...], ht, preferred_element_type=F32).astype(BF16)

    def head(hd, carry):
        q1 = q_ref[pl.ds(pl.multiple_of(hd * 2 * N_KEYS, 2 * N_KEYS), N_KEYS), :]
        q2 = q_ref[pl.ds(pl.multiple_of(hd * 2 * N_KEYS + N_KEYS, N_KEYS), N_KEYS), :]
        s1 = jnp.dot(keys_ref[2 * hd], q1, preferred_element_type=F32)
        s2 = jnp.dot(keys_ref[2 * hd + 1], q2, preferred_element_type=F32)
        for c in range(tf // LANES):
            cols = slice(c * LANES, (c + 1) * LANES)
            s1c = s1[:, cols]
            s2c = s2[:, cols]
            v1, coded1 = _top16(s1c)
            v2, coded2 = _top16(s2c)
            cnt, z = _pair_counts(v1, v2)
            cnt_i = jnp.zeros_like(coded1)
            for r in range(PEER_TOPK):
                cnt_i = jnp.where(coded1 == _rank_code(r), cnt[r], cnt_i)
            rank2 = jnp.where(coded2 < -0.5 * RANK_BASE,
                              (-coded2 - RANK_BASE) * (1.0 / RANK_STEP), UNRANKED)
            cnt_ref[hd, :, cols] = cnt_i
            e1_ref[hd, :, cols] = jnp.exp(s1c - v1[0]) / z
            rk_ref[hd, :, cols] = rank2.astype(BF16)
            e2_ref[hd, :, cols] = jnp.exp(s2c - v2[0]).astype(BF16)
        return carry

    lax.fori_loop(0, PEER_HEADS, head, 0, unroll=True)


def _peer_route(x1, g, wq_t, keys, tf):
    s, d = x1.shape
    qw = wq_t.shape[0]
    tab = lambda dt: jax.ShapeDtypeStruct((PEER_HEADS, N_KEYS, s), dt)
    tab_spec = pl.BlockSpec((PEER_HEADS, N_KEYS, tf), lambda i: (0, 0, i))
    return pl.pallas_call(
        functools.partial(_peer_route_body, tf=tf),
        grid=(s // tf,),
        in_specs=[pl.BlockSpec((tf, d), lambda i: (i, 0)),
                  pl.BlockSpec((1, d), lambda i: (0, 0)),
                  pl.BlockSpec((qw, d), lambda i: (0, 0)),
                  pl.BlockSpec(keys.shape, lambda i: (0, 0, 0))],
        out_specs=(pl.BlockSpec((d, tf), lambda i: (0, i)),
                   tab_spec, tab_spec, tab_spec, tab_spec),
        out_shape=(jax.ShapeDtypeStruct((d, s), BF16), tab(F32), tab(F32), tab(BF16), tab(BF16)),
        scratch_shapes=[pltpu.VMEM((qw, tf), BF16)],
        compiler_params=_params(1),
        name="peer_route",
    )(x1, g, wq_t, keys)


def _peer_experts_body(ht_ref, u_ref, vt_ref, cnt_ref, e1_ref, rk_ref, e2_ref, x_ref, o_ref,
                       acc_ref, *, rows_per_chunk, n_parts):
    c = pl.program_id(1)
    tm = ht_ref.shape[1]
    rows_per_part = rows_per_chunk // n_parts

    @pl.when(c == 0)
    def _():
        acc_ref[...] = jnp.zeros_like(acc_ref)

    ht = ht_ref[...]

    def part_rows(p):
        return slice(p * rows_per_part * N_KEYS, (p + 1) * rows_per_part * N_KEYS)

    def project(p):
        return jnp.dot(u_ref[part_rows(p), :], ht, preferred_element_type=F32)

    def activate(p, a):
        acts = []
        for k in range(rows_per_part):
            r = p * rows_per_part + k
            ar = a[k * N_KEYS:(k + 1) * N_KEYS]
            gel = (0.5 * ar * (1.0 + lax.erf(ar * (2.0 ** -0.5)))).astype(BF16)
            gate = None
            for hd in range(PEER_HEADS):
                cnt = jnp.broadcast_to(cnt_ref[hd, r:r + 1, :], (N_KEYS, tm)).astype(BF16)
                e1 = jnp.broadcast_to(e1_ref[hd, r:r + 1, :], (N_KEYS, tm)).astype(BF16)
                e2 = e2_ref[hd]
                term = jnp.where(rk_ref[hd] < cnt, e2, jnp.zeros_like(e2)) * e1
                gate = term if gate is None else gate + term
            acts.append(gel * gate)
        return jnp.concatenate(acts, axis=0)

    act = jnp.concatenate([activate(p, project(p)) for p in range(n_parts)], axis=0)
    acc_ref[...] += lax.dot_general(act, vt_ref[...], (((0,), (0,)), ((), ())),
                                    preferred_element_type=F32)

    @pl.when(c == pl.num_programs(1) - 1)
    def _():
        o_ref[...] = x_ref[...] + acc_ref[...]


def _peer_experts(ht, u, vt, cnt, e1, rk, e2, x1, tm, ec, n_parts):
    s, d = x1.shape
    n_exp = u.shape[0]
    rpc = ec // N_KEYS
    tab = pl.BlockSpec((PEER_HEADS, N_KEYS, tm), lambda i, c: (0, 0, i))
    row = pl.BlockSpec((PEER_HEADS, rpc, tm), lambda i, c: (0, c, i))
    return pl.pallas_call(
        functools.partial(_peer_experts_body, rows_per_chunk=rpc, n_parts=n_parts),
        grid=(s // tm, n_exp // ec),
        in_specs=[pl.BlockSpec((d, tm), lambda i, c: (0, i)),
                  pl.BlockSpec((ec, d), lambda i, c: (c, 0)),
                  pl.BlockSpec((ec, d), lambda i, c: (c, 0)),
                  row, row, tab, tab,
                  pl.BlockSpec((tm, d), lambda i, c: (i, 0))],
        out_specs=pl.BlockSpec((tm, d), lambda i, c: (i, 0)),
        out_shape=jax.ShapeDtypeStruct((s, d), F32),
        scratch_shapes=[pltpu.VMEM((tm, d), F32)],
        compiler_params=_params(2),
        name="peer_experts",
    )(ht, u, vt, cnt, e1, rk, e2, x1)


class _Tiles(NamedTuple):
    in_proj_m: int
    in_proj_n: int
    mixer: int
    route: int
    experts_m: int
    experts_chunk: int
    experts_parts: int


def _tiles(s):
    return _Tiles(in_proj_m=min(s, 1024), in_proj_n=1536, mixer=min(s, 512),
                  route=min(s, 256), experts_m=min(s, 512), experts_chunk=1024,
                  experts_parts=8)


def kernel(x, mem, positions, g_mix, w_in, q_norm_g, k_norm_g, attn_sinks, w_attn_o,
           conv_dw_w, conv_dw_b, conv_ln_g, conv_ln_b, w_conv_o, g_mem, w_mem_kv,
           mq_norm_g, mk_norm_g, w_mem_o, w_out, g_ffn, w_query, sub_keys, expert_u, expert_v):
    b, s, d = x.shape
    depth = g_mix.shape[0]
    q_w = N_HEADS * HEAD_DIM
    kv_w = N_KV_HEADS * HEAD_DIM
    conv_ch = conv_dw_b.shape[-1]
    mem_w = MEM_HEADS * MEM_HEAD_DIM
    conv_col = q_w + 2 * kv_w
    mq_col = conv_col + 2 * conv_ch
    gate_col = mq_col + mem_w
    t = _tiles(s)

    row = lambda v: v.reshape(1, -1).astype(F32)
    inv_freq = ROPE_THETA ** (-jnp.arange(0, HEAD_DIM, 2, dtype=F32) / HEAD_DIM)
    invf = jnp.tile(inv_freq, LANES // (HEAD_DIM // 2)).reshape(1, LANES)
    head_of_lane = jnp.arange(LANES) // HEAD_DIM
    bd = (head_of_lane[:, None] == head_of_lane[None, :]).astype(BF16) * (1.0 / HEAD_DIM)

    outs = []
    for bi in range(b):
        xb = x[bi]
        pos_b = jnp.broadcast_to(positions[bi].astype(F32)[:, None], (s, LANES))
        for l in range(depth):
            proj = _in_proj(xb, row(g_mix[l]), w_in[l].astype(BF16), t.in_proj_m, t.in_proj_n)
            attn = _swa(proj, pos_b, invf, jnp.tile(row(q_norm_g[l]), (1, 2)),
                        jnp.tile(row(k_norm_g[l]), (1, 2)), bd, attn_sinks[l].astype(F32),
                        t.mixer)
            conv = _conv(proj, conv_dw_w[l].reshape(CONV_WIDTH, conv_ch), row(conv_dw_b[l]),
                         row(conv_ln_g[l]), row(conv_ln_b[l]), t.mixer, conv_col, conv_ch)
            mk, mv = _mem_kv(mem[bi], row(g_mem[l]), w_mem_kv[l].astype(BF16),
                             row(mk_norm_g[l]))
            memo = _mem_attn(proj, mk, mv, row(mq_norm_g[l]), t.mixer, mq_col)
            x1 = _mix_out(xb, attn, conv, memo, proj, w_attn_o[l].astype(BF16),
                          w_conv_o[l].astype(BF16), w_mem_o[l].astype(BF16),
                          w_out[l].astype(BF16), t.mixer, gate_col)
            keys = sub_keys[l].reshape(2 * PEER_HEADS, N_KEYS, -1).astype(BF16)
            ht, cnt, e1, rk, e2 = _peer_route(x1, row(g_ffn[l]), w_query[l].astype(BF16).T,
                                              keys, t.route)
            xb = _peer_experts(ht, expert_u[l].astype(BF16), expert_v[l].astype(BF16),
                               cnt, e1, rk, e2, x1, t.experts_m, t.experts_chunk,
                               t.experts_parts)
        outs.append(xb)
    return jnp.stack(outs)
```

```python
import functools
from typing import NamedTuple

import jax
import jax.numpy as jnp
from jax import lax
from jax.experimental import pallas as pl
from jax.experimental.pallas import tpu as pltpu

F32 = jnp.float32
BF16 = jnp.bfloat16

EPS = 1e-6
NEG = -1e30
LANES = 128
SUBLANES = 8
HEAD_DIM = 64
N_HEADS = 16
N_KV_HEADS = 4
ATTN_BLOCK = 128
ROPE_THETA = 10000.0
CONV_WIDTH = 31
CONV_HALO = 32
MEM_HEADS = 4
MEM_HEAD_DIM = 128
PEER_HEADS = 8
N_KEYS = 128
PEER_TOPK = 16
UNRANKED = 64.0
RANK_BASE = 2.0 ** 100
RANK_STEP = 2.0 ** 77

VMEM_LIMIT = 56 * 1024 * 1024


def _params(n_axes, vmem=VMEM_LIMIT, **kwargs):
    return pltpu.CompilerParams(
        dimension_semantics=("arbitrary",) * n_axes, vmem_limit_bytes=vmem, **kwargs)


def _in_proj_body(x_ref, g_ref, w_ref, o_ref, h_ref):
    @pl.when(pl.program_id(1) == 0)
    def _():
        x = x_ref[...]
        ms = jnp.mean(x * x, axis=-1, keepdims=True)
        h_ref[...] = (x * lax.rsqrt(ms + EPS) * g_ref[...]).astype(BF16)

    o_ref[...] = jnp.dot(h_ref[...], w_ref[...],
                         preferred_element_type=F32).astype(o_ref.dtype)


def _in_proj(x, g, w, tm, tn):
    s, d = x.shape
    n = w.shape[1]
    return pl.pallas_call(
        _in_proj_body,
        grid=(s // tm, n // tn),
        in_specs=[pl.BlockSpec((tm, d), lambda i, j: (i, 0)),
                  pl.BlockSpec((1, d), lambda i, j: (0, 0)),
                  pl.BlockSpec((d, tn), lambda i, j: (0, j))],
        out_specs=pl.BlockSpec((tm, tn), lambda i, j: (i, j)),
        out_shape=jax.ShapeDtypeStruct((s, n), BF16),
        scratch_shapes=[pltpu.VMEM((tm, d), BF16)],
        compiler_params=_params(2),
        name="in_proj",
    )(x, g, w)


def _swa_body(sinks_ref, q_ref, kv_ref, kvp_ref, pos_ref, posp_ref, invf_ref, gq_ref,
              gk_ref, bd_ref, o_ref, klo_ref, khi_ref, vlo_ref, vhi_ref, bias_ref, *, tq):
    i = pl.program_id(0)
    nblk = tq // ATTN_BLOCK
    lane = lax.broadcasted_iota(jnp.int32, (1, LANES), 1)
    first_half = (lane % HEAD_DIM) < (HEAD_DIM // 2)
    low_head = lane < HEAD_DIM
    invf = invf_ref[...]
    bd = bd_ref[...]

    def rope_tables(pos):
        ang = pos * invf
        sin = jnp.sin(ang)
        return jnp.cos(ang), jnp.where(first_half, -sin, sin)

    def norm_rope(xf, g, cos, sin_signed):
        ms = jnp.dot((xf * xf).astype(BF16), bd, preferred_element_type=F32)
        xn = xf * lax.rsqrt(ms + EPS) * g
        swapped = jnp.where(first_half, pltpu.roll(xn, LANES - HEAD_DIM // 2, 1),
                            pltpu.roll(xn, HEAD_DIM // 2, 1))
        return xn * cos + swapped * sin_signed

    cos_c, sin_c = rope_tables(pos_ref[...])
    cos_p, sin_p = rope_tables(posp_ref[...])
    gk = gk_ref[...]

    def stage_kv(kv_tile, cos, sin_signed, row0, rows):
        for kc in range(N_KV_HEADS // 2):
            kf = kv_tile[:, kc * LANES:(kc + 1) * LANES].astype(F32)
            kr = norm_rope(kf, gk, cos, sin_signed)
            vf = kv_tile[:, (2 + kc) * LANES:(3 + kc) * LANES].astype(F32)
            for src, lo_ref, hi_ref in ((kr, klo_ref, khi_ref), (vf, vlo_ref, vhi_ref)):
                rolled = pltpu.roll(src, HEAD_DIM, 1)
                zero = jnp.zeros_like(src)
                sl = pl.ds(row0, rows)
                lo_ref[2 * kc, sl, :] = jnp.where(low_head, src, zero).astype(BF16)
                hi_ref[2 * kc, sl, :] = jnp.where(low_head, zero, rolled).astype(BF16)
                lo_ref[2 * kc + 1, sl, :] = jnp.where(low_head, rolled, zero).astype(BF16)
                hi_ref[2 * kc + 1, sl, :] = jnp.where(low_head, zero, src).astype(BF16)

    stage_kv(kvp_ref[...], cos_p, sin_p, 0, ATTN_BLOCK)
    stage_kv(kv_ref[...], cos_c, sin_c, ATTN_BLOCK, tq)

    qi = lax.broadcasted_iota(jnp.int32, (tq, 2 * ATTN_BLOCK), 0)
    kj = lax.broadcasted_iota(jnp.int32, (tq, 2 * ATTN_BLOCK), 1)
    diff = qi % ATTN_BLOCK + ATTN_BLOCK - kj
    first_key = jnp.where(i > 0, 0, ATTN_BLOCK)
    visible = ((diff >= 0) & (diff < ATTN_BLOCK)
               & ((kj >= first_key) | (qi >= ATTN_BLOCK)))
    bias_ref[...] = jnp.where(visible, 0.0, NEG)
    gq = gq_ref[...]
    ones = jnp.ones((2 * ATTN_BLOCK, LANES), BF16)
    nt = (((1,), (1,)), ((), ()))
    blocks = [(slice(n * ATTN_BLOCK, (n + 1) * ATTN_BLOCK),
               pl.ds(n * ATTN_BLOCK, 2 * ATTN_BLOCK)) for n in range(nblk)]

    for pc in range(N_HEADS // 2):
        g = pc // 2
        qf = q_ref[:, pc * LANES:(pc + 1) * LANES].astype(F32)
        qr = (norm_rope(qf, gq, cos_c, sin_c) * (HEAD_DIM ** -0.5)).astype(BF16)
        out = None
        for hh, (k_ref, v_ref) in enumerate(((klo_ref, vlo_ref), (khi_ref, vhi_ref))):
            s = jnp.concatenate(
                [lax.dot_general(qr[rows], k_ref[g, keys, :], nt, preferred_element_type=F32)
                 for rows, keys in blocks], axis=0) + bias_ref[...]
            sink = sinks_ref[2 * pc + hh]
            m = jnp.maximum(jnp.max(s, axis=-1, keepdims=True), sink)
            e = jnp.exp(s - m).astype(BF16)
            den = jnp.dot(e, ones, preferred_element_type=F32) + jnp.exp(sink - m)
            o = jnp.concatenate(
                [jnp.dot(e[rows], v_ref[g, keys, :], preferred_element_type=F32)
                 for rows, keys in blocks], axis=0) / den
            out = o if out is None else out + o
        o_ref[:, pc * LANES:(pc + 1) * LANES] = out.astype(o_ref.dtype)


def _swa(proj, pos_b, invf, gq, gk, bd, sinks, tq):
    s = proj.shape[0]
    rpb = tq // ATTN_BLOCK
    kv_blk = (N_HEADS * HEAD_DIM) // (2 * N_KV_HEADS * HEAD_DIM)
    prev = lambda i: jnp.maximum(i * rpb - 1, 0)
    rows = tq + ATTN_BLOCK
    stage = pltpu.VMEM((N_KV_HEADS, rows, LANES), BF16)
    return pl.pallas_call(
        functools.partial(_swa_body, tq=tq),
        grid=(s // tq,),
        in_specs=[pl.BlockSpec(memory_space=pltpu.SMEM),
                  pl.BlockSpec((tq, N_HEADS * HEAD_DIM), lambda i: (i, 0)),
                  pl.BlockSpec((tq, 2 * N_KV_HEADS * HEAD_DIM), lambda i: (i, kv_blk)),
                  pl.BlockSpec((ATTN_BLOCK, 2 * N_KV_HEADS * HEAD_DIM),
                               lambda i: (prev(i), kv_blk)),
                  pl.BlockSpec((tq, LANES), lambda i: (i, 0)),
                  pl.BlockSpec((ATTN_BLOCK, LANES), lambda i: (prev(i), 0)),
                  pl.BlockSpec((1, LANES), lambda i: (0, 0)),
                  pl.BlockSpec((1, LANES), lambda i: (0, 0)),
                  pl.BlockSpec((1, LANES), lambda i: (0, 0)),
                  pl.BlockSpec((LANES, LANES), lambda i: (0, 0))],
        out_specs=pl.BlockSpec((tq, N_HEADS * HEAD_DIM), lambda i: (i, 0)),
        out_shape=jax.ShapeDtypeStruct((s, N_HEADS * HEAD_DIM), BF16),
        scratch_shapes=[stage, stage, stage, stage,
                        pltpu.VMEM((tq, 2 * ATTN_BLOCK), F32)],
        compiler_params=_params(1),
        name="swa",
    )(sinks, proj, proj, proj, pos_b, pos_b, invf, gq, gk, bd)


def _conv_body(a_ref, b_ref, ah_ref, bh_ref, w_ref, b0_ref, lg_ref, lb_ref, o_ref,
               ext_ref, shift_ref, c_ref, *, tm):
    i = pl.program_id(0)
    ext_ref[CONV_HALO:, :] = a_ref[...].astype(F32) * jax.nn.sigmoid(b_ref[...].astype(F32))
    halo = ah_ref[...].astype(F32) * jax.nn.sigmoid(bh_ref[...].astype(F32))
    ext_ref[:CONV_HALO, :] = jnp.where(i > 0, halo, jnp.zeros_like(halo))
    span = shift_ref.shape[1]
    for k in range(1, SUBLANES):
        shift_ref[k] = ext_ref[k:k + span, :]
    ch = ext_ref.shape[1]
    rows = 128
    for cc in range(ch // LANES):
        cols = slice(cc * LANES, (cc + 1) * LANES)
        for rc in range(tm // rows):
            acc = jnp.broadcast_to(b0_ref[:, cols], (rows, LANES))
            for w in range(CONV_WIDTH):
                start = rc * rows + CONV_HALO - (CONV_WIDTH - 1) + w
                k = start % SUBLANES
                if k == 0:
                    window = ext_ref[start:start + rows, cols]
                else:
                    window = shift_ref[k, start - k:start - k + rows, cols]
                acc = acc + window * w_ref[w:w + 1, cols]
            c_ref[rc * rows:(rc + 1) * rows, cols] = acc
    c = c_ref[...]
    mu = jnp.mean(c, axis=-1, keepdims=True)
    xc = c - mu
    y = xc * lax.rsqrt(jnp.mean(xc * xc, axis=-1, keepdims=True) + EPS)
    y = y * lg_ref[...] + lb_ref[...]
    o_ref[...] = (y * jax.nn.sigmoid(y)).astype(o_ref.dtype)


def _conv(proj, dw_w, dw_b, ln_g, ln_b, tm, a_col, ch):
    s = proj.shape[0]
    a_blk = a_col // ch
    hpb = tm // CONV_HALO
    prev = lambda i: jnp.maximum(i * hpb - 1, 0)
    vec = pl.BlockSpec((1, ch), lambda i: (0, 0))
    return pl.pallas_call(
        functools.partial(_conv_body, tm=tm),
        grid=(s // tm,),
        in_specs=[pl.BlockSpec((tm, ch), lambda i: (i, a_blk)),
                  pl.BlockSpec((tm, ch), lambda i: (i, a_blk + 1)),
                  pl.BlockSpec((CONV_HALO, ch), lambda i: (prev(i), a_blk)),
                  pl.BlockSpec((CONV_HALO, ch), lambda i: (prev(i), a_blk + 1)),
                  pl.BlockSpec((CONV_WIDTH, ch), lambda i: (0, 0)),
                  vec, vec, vec],
        out_specs=pl.BlockSpec((tm, ch), lambda i: (i, 0)),
        out_shape=jax.ShapeDtypeStruct((s, ch), BF16),
        scratch_shapes=[pltpu.VMEM((tm + CONV_HALO, ch), F32),
                        pltpu.VMEM((SUBLANES, tm + CONV_HALO - SUBLANES, ch), F32),
                        pltpu.VMEM((tm, ch), F32)],
        compiler_params=_params(1),
        name="conv",
    )(proj, proj, proj, proj, dw_w, dw_b, ln_g, ln_b)


def _mem_kv_body(mem_ref, g_ref, w_ref, kg_ref, k_ref, v_ref):
    m = mem_ref[...]
    ms = jnp.mean(m * m, axis=-1, keepdims=True)
    h = (m * lax.rsqrt(ms + EPS) * g_ref[...]).astype(BF16)
    kv = jnp.dot(h, w_ref[...], preferred_element_type=F32)
    width = MEM_HEADS * MEM_HEAD_DIM
    for hd in range(MEM_HEADS):
        cols = slice(hd * MEM_HEAD_DIM, (hd + 1) * MEM_HEAD_DIM)
        k = kv[:, cols]
        kms = jnp.mean(k * k, axis=-1, keepdims=True)
        k_ref[:, cols] = (k * lax.rsqrt(kms + EPS) * kg_ref[...]).astype(BF16)
    v_ref[...] = kv[:, width:].astype(BF16)


def _mem_kv(mem, g, w, kg):
    m = mem.shape[0]
    width = MEM_HEADS * MEM_HEAD_DIM
    out = jax.ShapeDtypeStruct((m, width), BF16)
    return pl.pallas_call(
        _mem_kv_body,
        out_shape=(out, out),
        compiler_params=pltpu.CompilerParams(vmem_limit_bytes=VMEM_LIMIT),
        name="mem_kv",
    )(mem, g, w, kg)


def _mem_attn_body(q_ref, k_ref, v_ref, qg_ref, o_ref):
    nt = (((1,), (1,)), ((), ()))
    for hd in range(MEM_HEADS):
        cols = slice(hd * MEM_HEAD_DIM, (hd + 1) * MEM_HEAD_DIM)
        q = q_ref[:, cols].astype(F32)
        qms = jnp.mean(q * q, axis=-1, keepdims=True)
        qn = (q * lax.rsqrt(qms + EPS) * qg_ref[...]).astype(BF16)
        s = lax.dot_general(qn, k_ref[:, cols], nt, preferred_element_type=F32)
        s = s * (MEM_HEAD_DIM ** -0.5)
        e = jnp.exp(s - jnp.max(s, axis=-1, keepdims=True))
        p = (e / jnp.sum(e, axis=-1, keepdims=True)).astype(BF16)
        o_ref[:, cols] = jnp.dot(p, v_ref[:, cols],
                                 preferred_element_type=F32).astype(o_ref.dtype)


def _mem_attn(proj, mk, mv, qg, tm, q_col):
    s = proj.shape[0]
    width = MEM_HEADS * MEM_HEAD_DIM
    m = mk.shape[0]
    return pl.pallas_call(
        _mem_attn_body,
        grid=(s // tm,),
        in_specs=[pl.BlockSpec((tm, width), lambda i: (i, q_col // width)),
                  pl.BlockSpec((m, width), lambda i: (0, 0)),
                  pl.BlockSpec((m, width), lambda i: (0, 0)),
                  pl.BlockSpec((1, MEM_HEAD_DIM), lambda i: (0, 0))],
        out_specs=pl.BlockSpec((tm, width), lambda i: (i, 0)),
        out_shape=jax.ShapeDtypeStruct((s, width), BF16),
        compiler_params=_params(1),
        name="mem_attn",
    )(proj, mk, mv, qg)


def _mix_out_body(x_ref, at_ref, cv_ref, mm_ref, ga0, ga1, gc0, gc1, gm0, gm1,
                  wa_ref, wc_ref, wm_ref, wo_ref, o_ref):
    at = at_ref[...]
    cv = cv_ref[...]
    mm = mm_ref[...]
    acc = x_ref[...]
    half = wo_ref.shape[0] // 2
    for c, (ga, gc, gm) in enumerate(((ga0, gc0, gm0), (ga1, gc1, gm1))):
        cols = slice(c * half, (c + 1) * half)
        merged = jax.nn.sigmoid(ga[...].astype(F32)) * jnp.dot(
            at, wa_ref[:, cols], preferred_element_type=F32)
        merged += jax.nn.sigmoid(gc[...].astype(F32)) * jnp.dot(
            cv, wc_ref[:, cols], preferred_element_type=F32)
        merged += jax.nn.sigmoid(gm[...].astype(F32)) * jnp.dot(
            mm, wm_ref[:, cols], preferred_element_type=F32)
        acc = acc + jnp.dot(merged.astype(BF16), wo_ref[cols, :], preferred_element_type=F32)
    o_ref[...] = acc


def _mix_out(x, attn, conv, memo, proj, wa, wc, wm, wo, tm, gate_col):
    s, d = x.shape
    half = d // 2
    g0 = gate_col // half
    gate = lambda k: pl.BlockSpec((tm, half), lambda i: (i, g0 + k))
    full = lambda a: pl.BlockSpec(a.shape, lambda i: (0, 0))
    return pl.pallas_call(
        _mix_out_body,
        grid=(s // tm,),
        in_specs=[pl.BlockSpec((tm, d), lambda i: (i, 0)),
                  pl.BlockSpec((tm, attn.shape[1]), lambda i: (i, 0)),
                  pl.BlockSpec((tm, conv.shape[1]), lambda i: (i, 0)),
                  pl.BlockSpec((tm, memo.shape[1]), lambda i: (i, 0)),
                  gate(0), gate(1), gate(2), gate(3), gate(4), gate(5),
                  full(wa), full(wc), full(wm), full(wo)],
        out_specs=pl.BlockSpec((tm, d), lambda i: (i, 0)),
        out_shape=jax.ShapeDtypeStruct((s, d), F32),
        compiler_params=_params(1),
        name="mix_out",
    )(x, attn, conv, memo, proj, proj, proj, proj, proj, proj, wa, wc, wm, wo)


def _rank_code(r):
    return -(RANK_BASE + r * RANK_STEP)


def _top16(s):
    vals = []
    for r in range(PEER_TOPK):
        m = jnp.max(s, axis=0, keepdims=True)
        s = jnp.where(s == m, _rank_code(r), s)
        vals.append(m)
    return vals, s


def _pair_counts(v1, v2):
    shape = (SUBLANES, v1[0].shape[1])
    row = lax.broadcasted_iota(jnp.int32, shape, 0)
    rowf = row.astype(F32)
    v2lo = jnp.concatenate(v2[:8], axis=0)
    v2hi = jnp.concatenate(v2[8:], axis=0)
    ninf = jnp.full(shape, -jnp.inf, F32)

    def shifted(k):
        return pltpu.roll(v2lo, k, 0)

    groups = [v1[0] + v2lo, v1[0] + v2hi, v1[1] + v2lo,
              jnp.where(row < 5, v1[2] + v2lo, v1[4] + shifted(5)),
              jnp.where(row < 4, v1[3] + v2lo,
                        jnp.where(row < 6, v1[5] + shifted(4), v1[6] + shifted(6))),
              jnp.where(row < 2, v1[7] + v2lo, ninf),
              jnp.concatenate(v1[8:], axis=0) + v2[0]]
    index = [rowf, rowf + 8.0, rowf + 16.0,
             jnp.where(row < 5, rowf + 32.0, rowf + (64.0 - 5.0)),
             jnp.where(row < 4, rowf + 48.0,
                       jnp.where(row < 6, rowf + (80.0 - 4.0), rowf + (96.0 - 6.0))),
             rowf + 112.0,
             16.0 * (rowf + 8.0)]
    cand = jnp.concatenate(groups, axis=0)
    idx = jnp.concatenate(index, axis=0)
    remaining = cand
    best = v1[0] + v2[0]
    z = jnp.zeros_like(best)
    for _ in range(PEER_TOPK):
        m = jnp.max(remaining, axis=0, keepdims=True)
        first = jnp.min(jnp.where(remaining == m, idx, 4096.0), axis=0, keepdims=True)
        remaining = jnp.where(idx == first, -jnp.inf, remaining)
        z = z + jnp.exp(m - best)
    taken = jnp.where(remaining != cand, 1.0, 0.0)

    def count(group, lo, hi):
        rows = taken[SUBLANES * group:SUBLANES * (group + 1)]
        return jnp.sum(jnp.where((row >= lo) & (row < hi), rows, 0.0), axis=0, keepdims=True)

    cnt = [jnp.sum(taken[0:16], axis=0, keepdims=True), count(2, 0, 8),
           count(3, 0, 5), count(4, 0, 4), count(3, 5, 8), count(4, 4, 6), count(4, 6, 8),
           count(5, 0, 2)]
    cnt += [taken[48 + r:49 + r] for r in range(8)]
    return cnt, z


def _peer_route_body(x_ref, g_ref, wq_ref, keys_ref, ht_ref, cnt_ref, e1_ref, rk_ref, e2_ref,
                     q_ref, *, tf):
    x = x_ref[...]
    ms = jnp.mean(x * x, axis=-1, keepdims=True)
    ht = (x * lax.rsqrt(ms + EPS) * g_ref[...]).T.astype(BF16)
    ht_ref[...] = ht
    q_ref[...] = jnp.dot(wq_ref[...], ht, preferred_element_type=F32).astype(BF16)

    def head(hd, carry):
        q1 = q_ref[pl.ds(pl.multiple_of(hd * 2 * N_KEYS, 2 * N_KEYS), N_KEYS), :]
        q2 = q_ref[pl.ds(pl.multiple_of(hd * 2 * N_KEYS + N_KEYS, N_KEYS), N_KEYS), :]
        s1 = jnp.dot(keys_ref[2 * hd], q1, preferred_element_type=F32)
        s2 = jnp.dot(keys_ref[2 * hd + 1], q2, preferred_element_type=F32)
        for c in range(tf // LANES):
            cols = slice(c * LANES, (c + 1) * LANES)
            s1c = s1[:, cols]
            s2c = s2[:, cols]
            v1, coded1 = _top16(s1c)
            v2, coded2 = _top16(s2c)
            cnt, z = _pair_counts(v1, v2)
            cnt_i = jnp.zeros_like(coded1)
            for r in range(PEER_TOPK):
                cnt_i = jnp.where(coded1 == _rank_code(r), cnt[r], cnt_i)
            rank2 = jnp.where(coded2 < -0.5 * RANK_BASE,
                              (-coded2 - RANK_BASE) * (1.0 / RANK_STEP), UNRANKED)
            cnt_ref[hd, :, cols] = cnt_i
            e1_ref[hd, :, cols] = jnp.exp(s1c - v1[0]) / z
            rk_ref[hd, :, cols] = rank2.astype(BF16)
            e2_ref[hd, :, cols] = jnp.exp(s2c - v2[0]).astype(BF16)
        return carry

    lax.fori_loop(0, PEER_HEADS, head, 0, unroll=True)


def _peer_route(x1, g, wq_t, keys, tf):
    s, d = x1.shape
    qw = wq_t.shape[0]
    tab = lambda dt: jax.ShapeDtypeStruct((PEER_HEADS, N_KEYS, s), dt)
    tab_spec = pl.BlockSpec((PEER_HEADS, N_KEYS, tf), lambda i: (0, 0, i))
    return pl.pallas_call(
        functools.partial(_peer_route_body, tf=tf),
        grid=(s // tf,),
        in_specs=[pl.BlockSpec((tf, d), lambda i: (i, 0)),
                  pl.BlockSpec((1, d), lambda i: (0, 0)),
                  pl.BlockSpec((qw, d), lambda i: (0, 0)),
                  pl.BlockSpec(keys.shape, lambda i: (0, 0, 0))],
        out_specs=(pl.BlockSpec((d, tf), lambda i: (0, i)),
                   tab_spec, tab_spec, tab_spec, tab_spec),
        out_shape=(jax.ShapeDtypeStruct((d, s), BF16), tab(F32), tab(F32), tab(BF16), tab(BF16)),
        scratch_shapes=[pltpu.VMEM((qw, tf), BF16)],
        compiler_params=_params(1),
        name="peer_route",
    )(x1, g, wq_t, keys)


def _expert_vt_body(v_ref, o_ref):
    o_ref[0] = v_ref[...].T.astype(o_ref.dtype)


def _expert_vt(v, ec):
    n_exp, d = v.shape
    return pl.pallas_call(
        _expert_vt_body,
        grid=(n_exp // ec,),
        in_specs=[pl.BlockSpec((ec, d), lambda c: (c, 0))],
        out_specs=pl.BlockSpec((1, d, ec), lambda c: (c, 0, 0)),
        out_shape=jax.ShapeDtypeStruct((n_exp // ec, d, ec), BF16),
        compiler_params=_params(1),
        name="expert_vt",
    )(v)


def _peer_experts_body(ht_ref, u_ref, vt_ref, cnt_ref, e1_ref, rk_ref, e2_ref, x_ref, o_ref,
                       acc_ref, *, rows_per_chunk, n_parts):
    c = pl.program_id(1)
    tm = ht_ref.shape[1]
    rows_per_part = rows_per_chunk // n_parts

    @pl.when(c == 0)
    def _():
        acc_ref[...] = jnp.zeros_like(acc_ref)

    ht = ht_ref[...]

    def part_rows(p):
        return slice(p * rows_per_part * N_KEYS, (p + 1) * rows_per_part * N_KEYS)

    def project(p):
        return jnp.dot(u_ref[part_rows(p), :], ht, preferred_element_type=F32)

    def activate(p, a):
        acts = []
        for k in range(rows_per_part):
            r = p * rows_per_part + k
            ar = a[k * N_KEYS:(k + 1) * N_KEYS]
            gel = (0.5 * ar * (1.0 + lax.erf(ar * (2.0 ** -0.5)))).astype(BF16)
            gate = None
            for hd in range(PEER_HEADS):
                cnt = jnp.broadcast_to(cnt_ref[hd, r:r + 1, :], (N_KEYS, tm)).astype(BF16)
                e1 = jnp.broadcast_to(e1_ref[hd, r:r + 1, :], (N_KEYS, tm)).astype(BF16)
                e2 = e2_ref[hd]
                term = jnp.where(rk_ref[hd] < cnt, e2, jnp.zeros_like(e2)) * e1
                gate = term if gate is None else gate + term
            acts.append(gel * gate)
        return jnp.concatenate(acts, axis=0)

    act = jnp.concatenate([activate(p, project(p)) for p in range(n_parts)], axis=0)
    acc_ref[...] += jnp.dot(vt_ref[...], act, preferred_element_type=F32)

    @pl.when(c == pl.num_programs(1) - 1)
    def _():
        o_ref[...] = x_ref[...] + acc_ref[...].T


def _peer_experts(ht, u, vt, cnt, e1, rk, e2, x1, tm, ec, n_parts):
    s, d = x1.shape
    n_exp = u.shape[0]
    rpc = ec // N_KEYS
    tab = pl.BlockSpec((PEER_HEADS, N_KEYS, tm), lambda i, c: (0, 0, i))
    row = pl.BlockSpec((PEER_HEADS, rpc, tm), lambda i, c: (0, c, i))
    return pl.pallas_call(
        functools.partial(_peer_experts_body, rows_per_chunk=rpc, n_parts=n_parts),
        grid=(s // tm, n_exp // ec),
        in_specs=[pl.BlockSpec((d, tm), lambda i, c: (0, i)),
                  pl.BlockSpec((ec, d), lambda i, c: (c, 0)),
                  pl.BlockSpec((None, d, ec), lambda i, c: (c, 0, 0)),
                  row, row, tab, tab,
                  pl.BlockSpec((tm, d), lambda i, c: (i, 0))],
        out_specs=pl.BlockSpec((tm, d), lambda i, c: (i, 0)),
        out_shape=jax.ShapeDtypeStruct((s, d), F32),
        scratch_shapes=[pltpu.VMEM((d, tm), F32)],
        compiler_params=_params(2),
        name="peer_experts",
    )(ht, u, vt, cnt, e1, rk, e2, x1)


class _Tiles(NamedTuple):
    in_proj_m: int
    in_proj_n: int
    mixer: int
    route: int
    experts_m: int
    experts_chunk: int
    experts_parts: int


def _tiles(s):
    return _Tiles(in_proj_m=min(s, 1024), in_proj_n=1536, mixer=min(s, 512),
                  route=min(s, 256), experts_m=min(s, 512), experts_chunk=1024,
                  experts_parts=8)


def kernel(x, mem, positions, g_mix, w_in, q_norm_g, k_norm_g, attn_sinks, w_attn_o,
           conv_dw_w, conv_dw_b, conv_ln_g, conv_ln_b, w_conv_o, g_mem, w_mem_kv,
           mq_norm_g, mk_norm_g, w_mem_o, w_out, g_ffn, w_query, sub_keys, expert_u, expert_v):
    b, s, d = x.shape
    depth = g_mix.shape[0]
    q_w = N_HEADS * HEAD_DIM
    kv_w = N_KV_HEADS * HEAD_DIM
    conv_ch = conv_dw_b.shape[-1]
    mem_w = MEM_HEADS * MEM_HEAD_DIM
    conv_col = q_w + 2 * kv_w
    mq_col = conv_col + 2 * conv_ch
    gate_col = mq_col + mem_w
    t = _tiles(s)

    row = lambda v: v.reshape(1, -1).astype(F32)
    inv_freq = ROPE_THETA ** (-jnp.arange(0, HEAD_DIM, 2, dtype=F32) / HEAD_DIM)
    invf = jnp.tile(inv_freq, LANES // (HEAD_DIM // 2)).reshape(1, LANES)
    head_of_lane = jnp.arange(LANES) // HEAD_DIM
    bd = (head_of_lane[:, None] == head_of_lane[None, :]).astype(BF16) * (1.0 / HEAD_DIM)

    outs = []
    for bi in range(b):
        xb = x[bi]
        pos_b = jnp.broadcast_to(positions[bi].astype(F32)[:, None], (s, LANES))
        for l in range(depth):
            proj = _in_proj(xb, row(g_mix[l]), w_in[l].astype(BF16), t.in_proj_m, t.in_proj_n)
            attn = _swa(proj, pos_b, invf, jnp.tile(row(q_norm_g[l]), (1, 2)),
                        jnp.tile(row(k_norm_g[l]), (1, 2)), bd, attn_sinks[l].astype(F32),
                        t.mixer)
            conv = _conv(proj, conv_dw_w[l].reshape(CONV_WIDTH, conv_ch), row(conv_dw_b[l]),
                         row(conv_ln_g[l]), row(conv_ln_b[l]), t.mixer, conv_col, conv_ch)
            mk, mv = _mem_kv(mem[bi], row(g_mem[l]), w_mem_kv[l].astype(BF16),
                             row(mk_norm_g[l]))
            memo = _mem_attn(proj, mk, mv, row(mq_norm_g[l]), t.mixer, mq_col)
            x1 = _mix_out(xb, attn, conv, memo, proj, w_attn_o[l].astype(BF16),
                          w_conv_o[l].astype(BF16), w_mem_o[l].astype(BF16),
                          w_out[l].astype(BF16), t.mixer, gate_col)
            keys = sub_keys[l].reshape(2 * PEER_HEADS, N_KEYS, -1).astype(BF16)
            ht, cnt, e1, rk, e2 = _peer_route(x1, row(g_ffn[l]), w_query[l].astype(BF16).T,
                                              keys, t.route)
            xb = _peer_experts(ht, expert_u[l].astype(BF16),
                               _expert_vt(expert_v[l], t.experts_chunk),
                               cnt, e1, rk, e2, x1, t.experts_m, t.experts_chunk,
                               t.experts_parts)
        outs.append(xb)
    return jnp.stack(outs)
```

```python
import functools
from typing import NamedTuple

import jax
import jax.numpy as jnp
from jax import lax
from jax.experimental import pallas as pl
from jax.experimental.pallas import tpu as pltpu

F32 = jnp.float32
BF16 = jnp.bfloat16

EPS = 1e-6
NEG = -1e30
LANES = 128
SUBLANES = 8
HEAD_DIM = 64
N_HEADS = 16
N_KV_HEADS = 4
ATTN_BLOCK = 128
ROPE_THETA = 10000.0
CONV_WIDTH = 31
CONV_HALO = 32
MEM_HEADS = 4
MEM_HEAD_DIM = 128
PEER_HEADS = 8
N_KEYS = 128
PEER_TOPK = 16
UNRANKED = 64.0
RANK_BASE = 2.0 ** 100
RANK_STEP = 2.0 ** 77

VMEM_LIMIT = 56 * 1024 * 1024


def _params(n_axes, vmem=VMEM_LIMIT, **kwargs):
    return pltpu.CompilerParams(
        dimension_semantics=("arbitrary",) * n_axes, vmem_limit_bytes=vmem, **kwargs)


def _in_proj_body(x_ref, g_ref, w_ref, o_ref, h_ref):
    @pl.when(pl.program_id(1) == 0)
    def _():
        x = x_ref[...]
        ms = jnp.mean(x * x, axis=-1, keepdims=True)
        h_ref[...] = (x * lax.rsqrt(ms + EPS) * g_ref[...]).astype(BF16)

    o_ref[...] = jnp.dot(h_ref[...], w_ref[...],
                         preferred_element_type=F32).astype(o_ref.dtype)


def _in_proj(x, g, w, tm, tn):
    s, d = x.shape
    n = w.shape[1]
    return pl.pallas_call(
        _in_proj_body,
        grid=(s // tm, n // tn),
        in_specs=[pl.BlockSpec((tm, d), lambda i, j: (i, 0)),
                  pl.BlockSpec((1, d), lambda i, j: (0, 0)),
                  pl.BlockSpec((d, tn), lambda i, j: (0, j))],
        out_specs=pl.BlockSpec((tm, tn), lambda i, j: (i, j)),
        out_shape=jax.ShapeDtypeStruct((s, n), BF16),
        scratch_shapes=[pltpu.VMEM((tm, d), BF16)],
        compiler_params=_params(2),
        name="in_proj",
    )(x, g, w)


def _swa_body(sinks_ref, q_ref, kv_ref, kvp_ref, pos_ref, posp_ref, invf_ref, gq_ref,
              gk_ref, bd_ref, o_ref, klo_ref, khi_ref, vlo_ref, vhi_ref, bias_ref, *, tq):
    i = pl.program_id(0)
    nblk = tq // ATTN_BLOCK
    lane = lax.broadcasted_iota(jnp.int32, (1, LANES), 1)
    first_half = (lane % HEAD_DIM) < (HEAD_DIM // 2)
    low_head = lane < HEAD_DIM
    invf = invf_ref[...]
    bd = bd_ref[...]

    def rope_tables(pos):
        ang = pos * invf
        sin = jnp.sin(ang)
        return jnp.cos(ang), jnp.where(first_half, -sin, sin)

    def norm_rope(xf, g, cos, sin_signed):
        ms = jnp.dot((xf * xf).astype(BF16), bd, preferred_element_type=F32)
        xn = xf * lax.rsqrt(ms + EPS) * g
        swapped = jnp.where(first_half, pltpu.roll(xn, LANES - HEAD_DIM // 2, 1),
                            pltpu.roll(xn, HEAD_DIM // 2, 1))
        return xn * cos + swapped * sin_signed

    cos_c, sin_c = rope_tables(pos_ref[...])
    cos_p, sin_p = rope_tables(posp_ref[...])
    gk = gk_ref[...]

    def stage_kv(kv_tile, cos, sin_signed, row0, rows):
        for kc in range(N_KV_HEADS // 2):
            kf = kv_tile[:, kc * LANES:(kc + 1) * LANES].astype(F32)
            kr = norm_rope(kf, gk, cos, sin_signed)
            vf = kv_tile[:, (2 + kc) * LANES:(3 + kc) * LANES].astype(F32)
            for src, lo_ref, hi_ref in ((kr, klo_ref, khi_ref), (vf, vlo_ref, vhi_ref)):
                rolled = pltpu.roll(src, HEAD_DIM, 1)
                zero = jnp.zeros_like(src)
                sl = pl.ds(row0, rows)
                lo_ref[2 * kc, sl, :] = jnp.where(low_head, src, zero).astype(BF16)
                hi_ref[2 * kc, sl, :] = jnp.where(low_head, zero, rolled).astype(BF16)
                lo_ref[2 * kc + 1, sl, :] = jnp.where(low_head, rolled, zero).astype(BF16)
                hi_ref[2 * kc + 1, sl, :] = jnp.where(low_head, zero, src).astype(BF16)

    stage_kv(kvp_ref[...], cos_p, sin_p, 0, ATTN_BLOCK)
    stage_kv(kv_ref[...], cos_c, sin_c, ATTN_BLOCK, tq)

    qi = lax.broadcasted_iota(jnp.int32, (tq, 2 * ATTN_BLOCK), 0)
    kj = lax.broadcasted_iota(jnp.int32, (tq, 2 * ATTN_BLOCK), 1)
    diff = qi % ATTN_BLOCK + ATTN_BLOCK - kj
    first_key = jnp.where(i > 0, 0, ATTN_BLOCK)
    visible = ((diff >= 0) & (diff < ATTN_BLOCK)
               & ((kj >= first_key) | (qi >= ATTN_BLOCK)))
    bias_ref[...] = jnp.where(visible, 0.0, NEG)
    gq = gq_ref[...]
    ones = jnp.ones((2 * ATTN_BLOCK, LANES), BF16)
    nt = (((1,), (1,)), ((), ()))
    blocks = [(slice(n * ATTN_BLOCK, (n + 1) * ATTN_BLOCK),
               pl.ds(n * ATTN_BLOCK, 2 * ATTN_BLOCK)) for n in range(nblk)]

    for pc in range(N_HEADS // 2):
        g = pc // 2
        qf = q_ref[:, pc * LANES:(pc + 1) * LANES].astype(F32)
        qr = (norm_rope(qf, gq, cos_c, sin_c) * (HEAD_DIM ** -0.5)).astype(BF16)
        out = None
        for hh, (k_ref, v_ref) in enumerate(((klo_ref, vlo_ref), (khi_ref, vhi_ref))):
            s = jnp.concatenate(
                [lax.dot_general(qr[rows], k_ref[g, keys, :], nt, preferred_element_type=F32)
                 for rows, keys in blocks], axis=0) + bias_ref[...]
            sink = sinks_ref[2 * pc + hh]
            m = jnp.maximum(jnp.max(s, axis=-1, keepdims=True), sink)
            e = jnp.exp(s - m).astype(BF16)
            den = jnp.dot(e, ones, preferred_element_type=F32) + jnp.exp(sink - m)
            o = jnp.concatenate(
                [jnp.dot(e[rows], v_ref[g, keys, :], preferred_element_type=F32)
                 for rows, keys in blocks], axis=0) / den
            out = o if out is None else out + o
        o_ref[:, pc * LANES:(pc + 1) * LANES] = out.astype(o_ref.dtype)


def _swa(proj, pos_b, invf, gq, gk, bd, sinks, tq):
    s = proj.shape[0]
    rpb = tq // ATTN_BLOCK
    kv_blk = (N_HEADS * HEAD_DIM) // (2 * N_KV_HEADS * HEAD_DIM)
    prev = lambda i: jnp.maximum(i * rpb - 1, 0)
    rows = tq + ATTN_BLOCK
    stage = pltpu.VMEM((N_KV_HEADS, rows, LANES), BF16)
    return pl.pallas_call(
        functools.partial(_swa_body, tq=tq),
        grid=(s // tq,),
        in_specs=[pl.BlockSpec(memory_space=pltpu.SMEM),
                  pl.BlockSpec((tq, N_HEADS * HEAD_DIM), lambda i: (i, 0)),
                  pl.BlockSpec((tq, 2 * N_KV_HEADS * HEAD_DIM), lambda i: (i, kv_blk)),
                  pl.BlockSpec((ATTN_BLOCK, 2 * N_KV_HEADS * HEAD_DIM),
                               lambda i: (prev(i), kv_blk)),
                  pl.BlockSpec((tq, LANES), lambda i: (i, 0)),
                  pl.BlockSpec((ATTN_BLOCK, LANES), lambda i: (prev(i), 0)),
                  pl.BlockSpec((1, LANES), lambda i: (0, 0)),
                  pl.BlockSpec((1, LANES), lambda i: (0, 0)),
                  pl.BlockSpec((1, LANES), lambda i: (0, 0)),
                  pl.BlockSpec((LANES, LANES), lambda i: (0, 0))],
        out_specs=pl.BlockSpec((tq, N_HEADS * HEAD_DIM), lambda i: (i, 0)),
        out_shape=jax.ShapeDtypeStruct((s, N_HEADS * HEAD_DIM), BF16),
        scratch_shapes=[stage, stage, stage, stage,
                        pltpu.VMEM((tq, 2 * ATTN_BLOCK), F32)],
        compiler_params=_params(1),
        name="swa",
    )(sinks, proj, proj, proj, pos_b, pos_b, invf, gq, gk, bd)


def _conv_body(a_ref, b_ref, ah_ref, bh_ref, w_ref, b0_ref, lg_ref, lb_ref, o_ref,
               ext_ref, shift_ref, c_ref, *, tm):
    i = pl.program_id(0)
    ext_ref[CONV_HALO:, :] = a_ref[...].astype(F32) * jax.nn.sigmoid(b_ref[...].astype(F32))
    halo = ah_ref[...].astype(F32) * jax.nn.sigmoid(bh_ref[...].astype(F32))
    ext_ref[:CONV_HALO, :] = jnp.where(i > 0, halo, jnp.zeros_like(halo))
    span = shift_ref.shape[1]
    for k in range(1, SUBLANES):
        shift_ref[k] = ext_ref[k:k + span, :]
    ch = ext_ref.shape[1]
    rows = 128
    for cc in range(ch // LANES):
        cols = slice(cc * LANES, (cc + 1) * LANES)
        for rc in range(tm // rows):
            acc = jnp.broadcast_to(b0_ref[:, cols], (rows, LANES))
            for w in range(CONV_WIDTH):
                start = rc * rows + CONV_HALO - (CONV_WIDTH - 1) + w
                k = start % SUBLANES
                if k == 0:
                    window = ext_ref[start:start + rows, cols]
                else:
                    window = shift_ref[k, start - k:start - k + rows, cols]
                acc = acc + window * w_ref[w:w + 1, cols]
            c_ref[rc * rows:(rc + 1) * rows, cols] = acc
    c = c_ref[...]
    mu = jnp.mean(c, axis=-1, keepdims=True)
    xc = c - mu
    y = xc * lax.rsqrt(jnp.mean(xc * xc, axis=-1, keepdims=True) + EPS)
    y = y * lg_ref[...] + lb_ref[...]
    o_ref[...] = (y * jax.nn.sigmoid(y)).astype(o_ref.dtype)


def _conv(proj, dw_w, dw_b, ln_g, ln_b, tm, a_col, ch):
    s = proj.shape[0]
    a_blk = a_col // ch
    hpb = tm // CONV_HALO
    prev = lambda i: jnp.maximum(i * hpb - 1, 0)
    vec = pl.BlockSpec((1, ch), lambda i: (0, 0))
    return pl.pallas_call(
        functools.partial(_conv_body, tm=tm),
        grid=(s // tm,),
        in_specs=[pl.BlockSpec((tm, ch), lambda i: (i, a_blk)),
                  pl.BlockSpec((tm, ch), lambda i: (i, a_blk + 1)),
                  pl.BlockSpec((CONV_HALO, ch), lambda i: (prev(i), a_blk)),
                  pl.BlockSpec((CONV_HALO, ch), lambda i: (prev(i), a_blk + 1)),
                  pl.BlockSpec((CONV_WIDTH, ch), lambda i: (0, 0)),
                  vec, vec, vec],
        out_specs=pl.BlockSpec((tm, ch), lambda i: (i, 0)),
        out_shape=jax.ShapeDtypeStruct((s, ch), BF16),
        scratch_shapes=[pltpu.VMEM((tm + CONV_HALO, ch), F32),
                        pltpu.VMEM((SUBLANES, tm + CONV_HALO - SUBLANES, ch), F32),
                        pltpu.VMEM((tm, ch), F32)],
        compiler_params=_params(1),
        name="conv",
    )(proj, proj, proj, proj, dw_w, dw_b, ln_g, ln_b)


def _mem_kv_body(mem_ref, g_ref, w_ref, kg_ref, k_ref, v_ref):
    m = mem_ref[...]
    ms = jnp.mean(m * m, axis=-1, keepdims=True)
    h = (m * lax.rsqrt(ms + EPS) * g_ref[...]).astype(BF16)
    kv = jnp.dot(h, w_ref[...], preferred_element_type=F32)
    width = MEM_HEADS * MEM_HEAD_DIM
    for hd in range(MEM_HEADS):
        cols = slice(hd * MEM_HEAD_DIM, (hd + 1) * MEM_HEAD_DIM)
        k = kv[:, cols]
        kms = jnp.mean(k * k, axis=-1, keepdims=True)
        k_ref[:, cols] = (k * lax.rsqrt(kms + EPS) * kg_ref[...]).astype(BF16)
    v_ref[...] = kv[:, width:].astype(BF16)


def _mem_kv(mem, g, w, kg):
    m = mem.shape[0]
    width = MEM_HEADS * MEM_HEAD_DIM
    out = jax.ShapeDtypeStruct((m, width), BF16)
    return pl.pallas_call(
        _mem_kv_body,
        out_shape=(out, out),
        compiler_params=pltpu.CompilerParams(vmem_limit_bytes=VMEM_LIMIT),
        name="mem_kv",
    )(mem, g, w, kg)


def _mem_attn_body(q_ref, k_ref, v_ref, qg_ref, o_ref):
    nt = (((1,), (1,)), ((), ()))
    for hd in range(MEM_HEADS):
        cols = slice(hd * MEM_HEAD_DIM, (hd + 1) * MEM_HEAD_DIM)
        q = q_ref[:, cols].astype(F32)
        qms = jnp.mean(q * q, axis=-1, keepdims=True)
        qn = (q * lax.rsqrt(qms + EPS) * qg_ref[...]).astype(BF16)
        s = lax.dot_general(qn, k_ref[:, cols], nt, preferred_element_type=F32)
        s = s * (MEM_HEAD_DIM ** -0.5)
        e = jnp.exp(s - jnp.max(s, axis=-1, keepdims=True))
        p = (e / jnp.sum(e, axis=-1, keepdims=True)).astype(BF16)
        o_ref[:, cols] = jnp.dot(p, v_ref[:, cols],
                                 preferred_element_type=F32).astype(o_ref.dtype)


def _mem_attn(proj, mk, mv, qg, tm, q_col):
    s = proj.shape[0]
    width = MEM_HEADS * MEM_HEAD_DIM
    m = mk.shape[0]
    return pl.pallas_call(
        _mem_attn_body,
        grid=(s // tm,),
        in_specs=[pl.BlockSpec((tm, width), lambda i: (i, q_col // width)),
                  pl.BlockSpec((m, width), lambda i: (0, 0)),
                  pl.BlockSpec((m, width), lambda i: (0, 0)),
                  pl.BlockSpec((1, MEM_HEAD_DIM), lambda i: (0, 0))],
        out_specs=pl.BlockSpec((tm, width), lambda i: (i, 0)),
        out_shape=jax.ShapeDtypeStruct((s, width), BF16),
        compiler_params=_params(1),
        name="mem_attn",
    )(proj, mk, mv, qg)


def _mix_out_body(x_ref, at_ref, cv_ref, mm_ref, ga0, ga1, gc0, gc1, gm0, gm1,
                  wa_ref, wc_ref, wm_ref, wo_ref, o_ref):
    at = at_ref[...]
    cv = cv_ref[...]
    mm = mm_ref[...]
    acc = x_ref[...]
    half = wo_ref.shape[0] // 2
    for c, (ga, gc, gm) in enumerate(((ga0, gc0, gm0), (ga1, gc1, gm1))):
        cols = slice(c * half, (c + 1) * half)
        merged = jax.nn.sigmoid(ga[...].astype(F32)) * jnp.dot(
            at, wa_ref[:, cols], preferred_element_type=F32)
        merged += jax.nn.sigmoid(gc[...].astype(F32)) * jnp.dot(
            cv, wc_ref[:, cols], preferred_element_type=F32)
        merged += jax.nn.sigmoid(gm[...].astype(F32)) * jnp.dot(
            mm, wm_ref[:, cols], preferred_element_type=F32)
        acc = acc + jnp.dot(merged.astype(BF16), wo_ref[cols, :], preferred_element_type=F32)
    o_ref[...] = acc


def _mix_out(x, attn, conv, memo, proj, wa, wc, wm, wo, tm, gate_col):
    s, d = x.shape
    half = d // 2
    g0 = gate_col // half
    gate = lambda k: pl.BlockSpec((tm, half), lambda i: (i, g0 + k))
    full = lambda a: pl.BlockSpec(a.shape, lambda i: (0, 0))
    return pl.pallas_call(
        _mix_out_body,
        grid=(s // tm,),
        in_specs=[pl.BlockSpec((tm, d), lambda i: (i, 0)),
                  pl.BlockSpec((tm, attn.shape[1]), lambda i: (i, 0)),
                  pl.BlockSpec((tm, conv.shape[1]), lambda i: (i, 0)),
                  pl.BlockSpec((tm, memo.shape[1]), lambda i: (i, 0)),
                  gate(0), gate(1), gate(2), gate(3), gate(4), gate(5),
                  full(wa), full(wc), full(wm), full(wo)],
        out_specs=pl.BlockSpec((tm, d), lambda i: (i, 0)),
        out_shape=jax.ShapeDtypeStruct((s, d), F32),
        compiler_params=_params(1),
        name="mix_out",
    )(x, attn, conv, memo, proj, proj, proj, proj, proj, proj, wa, wc, wm, wo)


def _rank_code(r):
    return -(RANK_BASE + r * RANK_STEP)


def _top16(s, key_index=None):
    vals = []
    for r in range(PEER_TOPK):
        m = jnp.max(s, axis=0, keepdims=True)
        hit = s == m
        if key_index is not None:
            first = jnp.min(jnp.where(hit, key_index, float(N_KEYS)), axis=0, keepdims=True)
            hit = key_index == first
        s = jnp.where(hit, _rank_code(r), s)
        vals.append(m)
    return vals, s


def _extracted(coded):
    return jnp.sum(jnp.where(coded < -0.5 * RANK_BASE, 1.0, 0.0), axis=0, keepdims=True)


def _pair_counts(v1, v2):
    shape = (SUBLANES, v1[0].shape[1])
    row = lax.broadcasted_iota(jnp.int32, shape, 0)
    rowf = row.astype(F32)
    v2lo = jnp.concatenate(v2[:8], axis=0)
    v2hi = jnp.concatenate(v2[8:], axis=0)
    ninf = jnp.full(shape, -jnp.inf, F32)

    def shifted(k):
        return pltpu.roll(v2lo, k, 0)

    groups = [v1[0] + v2lo, v1[0] + v2hi, v1[1] + v2lo,
              jnp.where(row < 5, v1[2] + v2lo, v1[4] + shifted(5)),
              jnp.where(row < 4, v1[3] + v2lo,
                        jnp.where(row < 6, v1[5] + shifted(4), v1[6] + shifted(6))),
              jnp.where(row < 2, v1[7] + v2lo, ninf),
              jnp.concatenate(v1[8:], axis=0) + v2[0]]
    index = [rowf, rowf + 8.0, rowf + 16.0,
             jnp.where(row < 5, rowf + 32.0, rowf + (64.0 - 5.0)),
             jnp.where(row < 4, rowf + 48.0,
                       jnp.where(row < 6, rowf + (80.0 - 4.0), rowf + (96.0 - 6.0))),
             rowf + 112.0,
             16.0 * (rowf + 8.0)]
    cand = jnp.concatenate(groups, axis=0)
    idx = jnp.concatenate(index, axis=0)
    remaining = cand
    best = v1[0] + v2[0]
    z = jnp.zeros_like(best)
    for _ in range(PEER_TOPK):
        m = jnp.max(remaining, axis=0, keepdims=True)
        first = jnp.min(jnp.where(remaining == m, idx, 4096.0), axis=0, keepdims=True)
        remaining = jnp.where(idx == first, -jnp.inf, remaining)
        z = z + jnp.exp(m - best)
    taken = jnp.where(remaining != cand, 1.0, 0.0)

    def count(group, lo, hi):
        rows = taken[SUBLANES * group:SUBLANES * (group + 1)]
        return jnp.sum(jnp.where((row >= lo) & (row < hi), rows, 0.0), axis=0, keepdims=True)

    cnt = [jnp.sum(taken[0:16], axis=0, keepdims=True), count(2, 0, 8),
           count(3, 0, 5), count(4, 0, 4), count(3, 5, 8), count(4, 4, 6), count(4, 6, 8),
           count(5, 0, 2)]
    cnt += [taken[48 + r:49 + r] for r in range(8)]
    return cnt, z


def _peer_route_body(x_ref, g_ref, wq_ref, keys_ref, ht_ref, cnt_ref, e1_ref, rk_ref, e2_ref,
                     q_ref, *, tf):
    x = x_ref[...]
    ms = jnp.mean(x * x, axis=-1, keepdims=True)
    ht = (x * lax.rsqrt(ms + EPS) * g_ref[...]).T.astype(BF16)
    ht_ref[...] = ht
    q_ref[...] = jnp.dot(wq_ref[...], ht, preferred_element_type=F32).astype(BF16)

    def head(hd, tie_safe):
        def q_rows(side):
            start = (2 * hd + side) * N_KEYS
            if not isinstance(hd, int):
                start = pl.multiple_of(start, N_KEYS)
            return q_ref[pl.ds(start, N_KEYS), :]

        s1 = jnp.dot(keys_ref[2 * hd], q_rows(0), preferred_element_type=F32)
        s2 = jnp.dot(keys_ref[2 * hd + 1], q_rows(1), preferred_element_type=F32)
        key_index = None
        if tie_safe:
            key_index = lax.broadcasted_iota(jnp.int32, (N_KEYS, LANES), 0).astype(F32)
        excess = jnp.zeros((1, LANES), F32)
        for c in range(tf // LANES):
            cols = slice(c * LANES, (c + 1) * LANES)
            s1c = s1[:, cols]
            s2c = s2[:, cols]
            v1, coded1 = _top16(s1c, key_index)
            v2, coded2 = _top16(s2c, key_index)
            cnt, z = _pair_counts(v1, v2)
            cnt_i = jnp.zeros_like(coded1)
            for r in range(PEER_TOPK):
                cnt_i = jnp.where(coded1 == _rank_code(r), cnt[r], cnt_i)
            rank2 = jnp.where(coded2 < -0.5 * RANK_BASE,
                              (-coded2 - RANK_BASE) * (1.0 / RANK_STEP), UNRANKED)
            cnt_ref[hd, :, cols] = cnt_i
            e1_ref[hd, :, cols] = jnp.exp(s1c - v1[0]) / z
            rk_ref[hd, :, cols] = rank2.astype(BF16)
            e2_ref[hd, :, cols] = jnp.exp(s2c - v2[0]).astype(BF16)
            if not tie_safe:
                excess = jnp.maximum(
                    excess, _extracted(coded1) + _extracted(coded2) - 2.0 * PEER_TOPK)
        return excess

    excess = head(0, False)
    for hd in range(1, PEER_HEADS):
        excess = jnp.maximum(excess, head(hd, False))

    @pl.when(jnp.max(excess) > 0.0)
    def _():
        def redo(hd, carry):
            head(hd, True)
            return carry
        lax.fori_loop(0, PEER_HEADS, redo, 0)


def _peer_route(x1, g, wq_t, keys, tf):
    s, d = x1.shape
    qw = wq_t.shape[0]
    tab = lambda dt: jax.ShapeDtypeStruct((PEER_HEADS, N_KEYS, s), dt)
    tab_spec = pl.BlockSpec((PEER_HEADS, N_KEYS, tf), lambda i: (0, 0, i))
    return pl.pallas_call(
        functools.partial(_peer_route_body, tf=tf),
        grid=(s // tf,),
        in_specs=[pl.BlockSpec((tf, d), lambda i: (i, 0)),
                  pl.BlockSpec((1, d), lambda i: (0, 0)),
                  pl.BlockSpec((qw, d), lambda i: (0, 0)),
                  pl.BlockSpec(keys.shape, lambda i: (0, 0, 0))],
        out_specs=(pl.BlockSpec((d, tf), lambda i: (0, i)),
                   tab_spec, tab_spec, tab_spec, tab_spec),
        out_shape=(jax.ShapeDtypeStruct((d, s), BF16), tab(F32), tab(F32), tab(BF16), tab(BF16)),
        scratch_shapes=[pltpu.VMEM((qw, tf), BF16)],
        compiler_params=_params(1),
        name="peer_route",
    )(x1, g, wq_t, keys)


def _expert_vt_body(v_ref, o_ref):
    o_ref[0] = v_ref[...].T.astype(o_ref.dtype)


def _expert_vt(v, ec):
    n_exp, d = v.shape
    return pl.pallas_call(
        _expert_vt_body,
        grid=(n_exp // ec,),
        in_specs=[pl.BlockSpec((ec, d), lambda c: (c, 0))],
        out_specs=pl.BlockSpec((1, d, ec), lambda c: (c, 0, 0)),
        out_shape=jax.ShapeDtypeStruct((n_exp // ec, d, ec), BF16),
        compiler_params=_params(1),
        name="expert_vt",
    )(v)


def _peer_experts_body(ht_ref, u_ref, vt_ref, cnt_ref, e1_ref, rk_ref, e2_ref, x_ref, o_ref,
                       acc_ref, *, rows_per_chunk, n_parts):
    c = pl.program_id(1)
    tm = ht_ref.shape[1]
    rows_per_part = rows_per_chunk // n_parts

    @pl.when(c == 0)
    def _():
        acc_ref[...] = jnp.zeros_like(acc_ref)

    ht = ht_ref[...]

    def part_rows(p):
        return slice(p * rows_per_part * N_KEYS, (p + 1) * rows_per_part * N_KEYS)

    def project(p):
        return jnp.dot(u_ref[part_rows(p), :], ht, preferred_element_type=F32)

    def activate(p, a):
        acts = []
        for k in range(rows_per_part):
            r = p * rows_per_part + k
            ar = a[k * N_KEYS:(k + 1) * N_KEYS]
            gel = (0.5 * ar * (1.0 + lax.erf(ar * (2.0 ** -0.5)))).astype(BF16)
            gate = None
            for hd in range(PEER_HEADS):
                cnt = jnp.broadcast_to(cnt_ref[hd, r:r + 1, :], (N_KEYS, tm)).astype(BF16)
                e1 = jnp.broadcast_to(e1_ref[hd, r:r + 1, :], (N_KEYS, tm)).astype(BF16)
                e2 = e2_ref[hd]
                term = jnp.where(rk_ref[hd] < cnt, e2, jnp.zeros_like(e2)) * e1
                gate = term if gate is None else gate + term
            acts.append(gel * gate)
        return jnp.concatenate(acts, axis=0)

    act = jnp.concatenate([activate(p, project(p)) for p in range(n_parts)], axis=0)
    acc_ref[...] += jnp.dot(vt_ref[...], act, preferred_element_type=F32)

    @pl.when(c == pl.num_programs(1) - 1)
    def _():
        o_ref[...] = x_ref[...] + acc_ref[...].T


def _peer_experts(ht, u, vt, cnt, e1, rk, e2, x1, tm, ec, n_parts):
    s, d = x1.shape
    n_exp = u.shape[0]
    rpc = ec // N_KEYS
    tab = pl.BlockSpec((PEER_HEADS, N_KEYS, tm), lambda i, c: (0, 0, i))
    row = pl.BlockSpec((PEER_HEADS, rpc, tm), lambda i, c: (0, c, i))
    return pl.pallas_call(
        functools.partial(_peer_experts_body, rows_per_chunk=rpc, n_parts=n_parts),
        grid=(s // tm, n_exp // ec),
        in_specs=[pl.BlockSpec((d, tm), lambda i, c: (0, i)),
                  pl.BlockSpec((ec, d), lambda i, c: (c, 0)),
                  pl.BlockSpec((None, d, ec), lambda i, c: (c, 0, 0)),
                  row, row, tab, tab,
                  pl.BlockSpec((tm, d), lambda i, c: (i, 0))],
        out_specs=pl.BlockSpec((tm, d), lambda i, c: (i, 0)),
        out_shape=jax.ShapeDtypeStruct((s, d), F32),
        scratch_shapes=[pltpu.VMEM((d, tm), F32)],
        compiler_params=_params(2),
        name="peer_experts",
    )(ht, u, vt, cnt, e1, rk, e2, x1)


class _Tiles(NamedTuple):
    in_proj_m: int
    in_proj_n: int
    mixer: int
    route: int
    experts_m: int
    experts_chunk: int
    experts_parts: int


def _tiles(s):
    return _Tiles(in_proj_m=min(s, 1024), in_proj_n=1536, mixer=min(s, 512),
                  route=min(s, 256), experts_m=min(s, 512), experts_chunk=1024,
                  experts_parts=8)


def kernel(x, mem, positions, g_mix, w_in, q_norm_g, k_norm_g, attn_sinks, w_attn_o,
           conv_dw_w, conv_dw_b, conv_ln_g, conv_ln_b, w_conv_o, g_mem, w_mem_kv,
           mq_norm_g, mk_norm_g, w_mem_o, w_out, g_ffn, w_query, sub_keys, expert_u, expert_v):
    b, s, d = x.shape
    depth = g_mix.shape[0]
    q_w = N_HEADS * HEAD_DIM
    kv_w = N_KV_HEADS * HEAD_DIM
    conv_ch = conv_dw_b.shape[-1]
    mem_w = MEM_HEADS * MEM_HEAD_DIM
    conv_col = q_w + 2 * kv_w
    mq_col = conv_col + 2 * conv_ch
    gate_col = mq_col + mem_w
    t = _tiles(s)

    row = lambda v: v.reshape(1, -1).astype(F32)
    inv_freq = ROPE_THETA ** (-jnp.arange(0, HEAD_DIM, 2, dtype=F32) / HEAD_DIM)
    invf = jnp.tile(inv_freq, LANES // (HEAD_DIM // 2)).reshape(1, LANES)
    head_of_lane = jnp.arange(LANES) // HEAD_DIM
    bd = (head_of_lane[:, None] == head_of_lane[None, :]).astype(BF16) * (1.0 / HEAD_DIM)

    outs = []
    for bi in range(b):
        xb = x[bi]
        pos_b = jnp.broadcast_to(positions[bi].astype(F32)[:, None], (s, LANES))
        for l in range(depth):
            proj = _in_proj(xb, row(g_mix[l]), w_in[l].astype(BF16), t.in_proj_m, t.in_proj_n)
            attn = _swa(proj, pos_b, invf, jnp.tile(row(q_norm_g[l]), (1, 2)),
                        jnp.tile(row(k_norm_g[l]), (1, 2)), bd, attn_sinks[l].astype(F32),
                        t.mixer)
            conv = _conv(proj, conv_dw_w[l].reshape(CONV_WIDTH, conv_ch), row(conv_dw_b[l]),
                         row(conv_ln_g[l]), row(conv_ln_b[l]), t.mixer, conv_col, conv_ch)
            mk, mv = _mem_kv(mem[bi], row(g_mem[l]), w_mem_kv[l].astype(BF16),
                             row(mk_norm_g[l]))
            memo = _mem_attn(proj, mk, mv, row(mq_norm_g[l]), t.mixer, mq_col)
            x1 = _mix_out(xb, attn, conv, memo, proj, w_attn_o[l].astype(BF16),
                          w_conv_o[l].astype(BF16), w_mem_o[l].astype(BF16),
                          w_out[l].astype(BF16), t.mixer, gate_col)
            keys = sub_keys[l].reshape(2 * PEER_HEADS, N_KEYS, -1).astype(BF16)
            ht, cnt, e1, rk, e2 = _peer_route(x1, row(g_ffn[l]), w_query[l].astype(BF16).T,
                                              keys, t.route)
            xb = _peer_experts(ht, expert_u[l].astype(BF16),
                               _expert_vt(expert_v[l], t.experts_chunk),
                               cnt, e1, rk, e2, x1, t.experts_m, t.experts_chunk,
                               t.experts_parts)
        outs.append(xb)
    return jnp.stack(outs)
```

```python
import functools
from typing import NamedTuple

import jax
import jax.numpy as jnp
from jax import lax
from jax.experimental import pallas as pl
from jax.experimental.pallas import tpu as pltpu

F32 = jnp.float32
BF16 = jnp.bfloat16

EPS = 1e-6
NEG = -1e30
LANES = 128
SUBLANES = 8
HEAD_DIM = 64
N_HEADS = 16
N_KV_HEADS = 4
ATTN_BLOCK = 128
ROPE_THETA = 10000.0
CONV_WIDTH = 31
CONV_HALO = 32
MEM_HEADS = 4
MEM_HEAD_DIM = 128
PEER_HEADS = 8
N_KEYS = 128
PEER_TOPK = 16
UNRANKED = 64.0
RANK_BASE = 2.0 ** 100
RANK_STEP = 2.0 ** 77

VMEM_LIMIT = 56 * 1024 * 1024


def _params(n_axes, vmem=VMEM_LIMIT, **kwargs):
    return pltpu.CompilerParams(
        dimension_semantics=("arbitrary",) * n_axes, vmem_limit_bytes=vmem, **kwargs)


def _in_proj_body(x_ref, g_ref, w_ref, o_ref, h_ref):
    @pl.when(pl.program_id(1) == 0)
    def _():
        x = x_ref[...]
        ms = jnp.mean(x * x, axis=-1, keepdims=True)
        h_ref[...] = (x * lax.rsqrt(ms + EPS) * g_ref[...]).astype(BF16)

    o_ref[...] = jnp.dot(h_ref[...], w_ref[...],
                         preferred_element_type=F32).astype(o_ref.dtype)


def _in_proj(x, g, w, tm, tn):
    s, d = x.shape
    n = w.shape[1]
    return pl.pallas_call(
        _in_proj_body,
        grid=(s // tm, n // tn),
        in_specs=[pl.BlockSpec((tm, d), lambda i, j: (i, 0)),
                  pl.BlockSpec((1, d), lambda i, j: (0, 0)),
                  pl.BlockSpec((d, tn), lambda i, j: (0, j))],
        out_specs=pl.BlockSpec((tm, tn), lambda i, j: (i, j)),
        out_shape=jax.ShapeDtypeStruct((s, n), BF16),
        scratch_shapes=[pltpu.VMEM((tm, d), BF16)],
        compiler_params=_params(2),
        name="in_proj",
    )(x, g, w)


def _swa_body(sinks_ref, q_ref, kv_ref, kvp_ref, pos_ref, posp_ref, invf_ref, gq_ref,
              gk_ref, bd_ref, o_ref, klo_ref, khi_ref, vlo_ref, vhi_ref, bias_ref, *, tq):
    i = pl.program_id(0)
    nblk = tq // ATTN_BLOCK
    lane = lax.broadcasted_iota(jnp.int32, (1, LANES), 1)
    first_half = (lane % HEAD_DIM) < (HEAD_DIM // 2)
    low_head = lane < HEAD_DIM
    invf = invf_ref[...]
    bd = bd_ref[...]

    def rope_tables(pos):
        ang = pos * invf
        sin = jnp.sin(ang)
        return jnp.cos(ang), jnp.where(first_half, -sin, sin)

    def norm_rope(xf, g, cos, sin_signed):
        ms = jnp.dot((xf * xf).astype(BF16), bd, preferred_element_type=F32)
        xn = xf * lax.rsqrt(ms + EPS) * g
        swapped = jnp.where(first_half, pltpu.roll(xn, LANES - HEAD_DIM // 2, 1),
                            pltpu.roll(xn, HEAD_DIM // 2, 1))
        return xn * cos + swapped * sin_signed

    cos_c, sin_c = rope_tables(pos_ref[...])
    cos_p, sin_p = rope_tables(posp_ref[...])
    gk = gk_ref[...]

    def stage_kv(kv_tile, cos, sin_signed, row0, rows):
        for kc in range(N_KV_HEADS // 2):
            kf = kv_tile[:, kc * LANES:(kc + 1) * LANES].astype(F32)
            kr = norm_rope(kf, gk, cos, sin_signed)
            vf = kv_tile[:, (2 + kc) * LANES:(3 + kc) * LANES].astype(F32)
            for src, lo_ref, hi_ref in ((kr, klo_ref, khi_ref), (vf, vlo_ref, vhi_ref)):
                rolled = pltpu.roll(src, HEAD_DIM, 1)
                zero = jnp.zeros_like(src)
                sl = pl.ds(row0, rows)
                lo_ref[2 * kc, sl, :] = jnp.where(low_head, src, zero).astype(BF16)
                hi_ref[2 * kc, sl, :] = jnp.where(low_head, zero, rolled).astype(BF16)
                lo_ref[2 * kc + 1, sl, :] = jnp.where(low_head, rolled, zero).astype(BF16)
                hi_ref[2 * kc + 1, sl, :] = jnp.where(low_head, zero, src).astype(BF16)

    stage_kv(kvp_ref[...], cos_p, sin_p, 0, ATTN_BLOCK)
    stage_kv(kv_ref[...], cos_c, sin_c, ATTN_BLOCK, tq)

    qi = lax.broadcasted_iota(jnp.int32, (tq, 2 * ATTN_BLOCK), 0)
    kj = lax.broadcasted_iota(jnp.int32, (tq, 2 * ATTN_BLOCK), 1)
    diff = qi % ATTN_BLOCK + ATTN_BLOCK - kj
    first_key = jnp.where(i > 0, 0, ATTN_BLOCK)
    visible = ((diff >= 0) & (diff < ATTN_BLOCK)
               & ((kj >= first_key) | (qi >= ATTN_BLOCK)))
    bias_ref[...] = jnp.where(visible, 0.0, NEG)
    gq = gq_ref[...]
    ones = jnp.ones((2 * ATTN_BLOCK, LANES), BF16)
    nt = (((1,), (1,)), ((), ()))
    blocks = [(slice(n * ATTN_BLOCK, (n + 1) * ATTN_BLOCK),
               pl.ds(n * ATTN_BLOCK, 2 * ATTN_BLOCK)) for n in range(nblk)]

    for pc in range(N_HEADS // 2):
        g = pc // 2
        qf = q_ref[:, pc * LANES:(pc + 1) * LANES].astype(F32)
        qr = (norm_rope(qf, gq, cos_c, sin_c) * (HEAD_DIM ** -0.5)).astype(BF16)
        out = None
        for hh, (k_ref, v_ref) in enumerate(((klo_ref, vlo_ref), (khi_ref, vhi_ref))):
            s = jnp.concatenate(
                [lax.dot_general(qr[rows], k_ref[g, keys, :], nt, preferred_element_type=F32)
                 for rows, keys in blocks], axis=0) + bias_ref[...]
            sink = sinks_ref[2 * pc + hh]
            m = jnp.maximum(jnp.max(s, axis=-1, keepdims=True), sink)
            e = jnp.exp(s - m).astype(BF16)
            den = jnp.dot(e, ones, preferred_element_type=F32) + jnp.exp(sink - m)
            o = jnp.concatenate(
                [jnp.dot(e[rows], v_ref[g, keys, :], preferred_element_type=F32)
                 for rows, keys in blocks], axis=0) / den
            out = o if out is None else out + o
        o_ref[:, pc * LANES:(pc + 1) * LANES] = out.astype(o_ref.dtype)


def _swa(proj, pos_b, invf, gq, gk, bd, sinks, tq):
    s = proj.shape[0]
    rpb = tq // ATTN_BLOCK
    kv_blk = (N_HEADS * HEAD_DIM) // (2 * N_KV_HEADS * HEAD_DIM)
    prev = lambda i: jnp.maximum(i * rpb - 1, 0)
    rows = tq + ATTN_BLOCK
    stage = pltpu.VMEM((N_KV_HEADS, rows, LANES), BF16)
    return pl.pallas_call(
        functools.partial(_swa_body, tq=tq),
        grid=(s // tq,),
        in_specs=[pl.BlockSpec(memory_space=pltpu.SMEM),
                  pl.BlockSpec((tq, N_HEADS * HEAD_DIM), lambda i: (i, 0)),
                  pl.BlockSpec((tq, 2 * N_KV_HEADS * HEAD_DIM), lambda i: (i, kv_blk)),
                  pl.BlockSpec((ATTN_BLOCK, 2 * N_KV_HEADS * HEAD_DIM),
                               lambda i: (prev(i), kv_blk)),
                  pl.BlockSpec((tq, LANES), lambda i: (i, 0)),
                  pl.BlockSpec((ATTN_BLOCK, LANES), lambda i: (prev(i), 0)),
                  pl.BlockSpec((1, LANES), lambda i: (0, 0)),
                  pl.BlockSpec((1, LANES), lambda i: (0, 0)),
                  pl.BlockSpec((1, LANES), lambda i: (0, 0)),
                  pl.BlockSpec((LANES, LANES), lambda i: (0, 0))],
        out_specs=pl.BlockSpec((tq, N_HEADS * HEAD_DIM), lambda i: (i, 0)),
        out_shape=jax.ShapeDtypeStruct((s, N_HEADS * HEAD_DIM), BF16),
        scratch_shapes=[stage, stage, stage, stage,
                        pltpu.VMEM((tq, 2 * ATTN_BLOCK), F32)],
        compiler_params=_params(1),
        name="swa",
    )(sinks, proj, proj, proj, pos_b, pos_b, invf, gq, gk, bd)


def _conv_body(a_ref, b_ref, ah_ref, bh_ref, w_ref, b0_ref, lg_ref, lb_ref, o_ref,
               ext_ref, shift_ref, c_ref, *, tm):
    i = pl.program_id(0)
    ext_ref[CONV_HALO:, :] = a_ref[...].astype(F32) * jax.nn.sigmoid(b_ref[...].astype(F32))
    halo = ah_ref[...].astype(F32) * jax.nn.sigmoid(bh_ref[...].astype(F32))
    ext_ref[:CONV_HALO, :] = jnp.where(i > 0, halo, jnp.zeros_like(halo))
    span = shift_ref.shape[1]
    for k in range(1, SUBLANES):
        shift_ref[k] = ext_ref[k:k + span, :]
    ch = ext_ref.shape[1]
    rows = 128
    for cc in range(ch // LANES):
        cols = slice(cc * LANES, (cc + 1) * LANES)
        for rc in range(tm // rows):
            acc = jnp.broadcast_to(b0_ref[:, cols], (rows, LANES))
            for w in range(CONV_WIDTH):
                start = rc * rows + CONV_HALO - (CONV_WIDTH - 1) + w
                k = start % SUBLANES
                if k == 0:
                    window = ext_ref[start:start + rows, cols]
                else:
                    window = shift_ref[k, start - k:start - k + rows, cols]
                acc = acc + window * w_ref[w:w + 1, cols]
            c_ref[rc * rows:(rc + 1) * rows, cols] = acc
    c = c_ref[...]
    mu = jnp.mean(c, axis=-1, keepdims=True)
    xc = c - mu
    y = xc * lax.rsqrt(jnp.mean(xc * xc, axis=-1, keepdims=True) + EPS)
    y = y * lg_ref[...] + lb_ref[...]
    o_ref[...] = (y * jax.nn.sigmoid(y)).astype(o_ref.dtype)


def _conv(proj, dw_w, dw_b, ln_g, ln_b, tm, a_col, ch):
    s = proj.shape[0]
    a_blk = a_col // ch
    hpb = tm // CONV_HALO
    prev = lambda i: jnp.maximum(i * hpb - 1, 0)
    vec = pl.BlockSpec((1, ch), lambda i: (0, 0))
    return pl.pallas_call(
        functools.partial(_conv_body, tm=tm),
        grid=(s // tm,),
        in_specs=[pl.BlockSpec((tm, ch), lambda i: (i, a_blk)),
                  pl.BlockSpec((tm, ch), lambda i: (i, a_blk + 1)),
                  pl.BlockSpec((CONV_HALO, ch), lambda i: (prev(i), a_blk)),
                  pl.BlockSpec((CONV_HALO, ch), lambda i: (prev(i), a_blk + 1)),
                  pl.BlockSpec((CONV_WIDTH, ch), lambda i: (0, 0)),
                  vec, vec, vec],
        out_specs=pl.BlockSpec((tm, ch), lambda i: (i, 0)),
        out_shape=jax.ShapeDtypeStruct((s, ch), BF16),
        scratch_shapes=[pltpu.VMEM((tm + CONV_HALO, ch), F32),
                        pltpu.VMEM((SUBLANES, tm + CONV_HALO - SUBLANES, ch), F32),
                        pltpu.VMEM((tm, ch), F32)],
        compiler_params=_params(1),
        name="conv",
    )(proj, proj, proj, proj, dw_w, dw_b, ln_g, ln_b)


def _mem_kv_body(mem_ref, g_ref, w_ref, kg_ref, k_ref, v_ref):
    m = mem_ref[...]
    ms = jnp.mean(m * m, axis=-1, keepdims=True)
    h = (m * lax.rsqrt(ms + EPS) * g_ref[...]).astype(BF16)
    kv = jnp.dot(h, w_ref[...], preferred_element_type=F32)
    width = MEM_HEADS * MEM_HEAD_DIM
    for hd in range(MEM_HEADS):
        cols = slice(hd * MEM_HEAD_DIM, (hd + 1) * MEM_HEAD_DIM)
        k = kv[:, cols]
        kms = jnp.mean(k * k, axis=-1, keepdims=True)
        k_ref[:, cols] = (k * lax.rsqrt(kms + EPS) * kg_ref[...]).astype(BF16)
    v_ref[...] = kv[:, width:].astype(BF16)


def _mem_kv(mem, g, w, kg):
    m = mem.shape[0]
    width = MEM_HEADS * MEM_HEAD_DIM
    out = jax.ShapeDtypeStruct((m, width), BF16)
    return pl.pallas_call(
        _mem_kv_body,
        out_shape=(out, out),
        compiler_params=pltpu.CompilerParams(vmem_limit_bytes=VMEM_LIMIT),
        name="mem_kv",
    )(mem, g, w, kg)


def _mem_attn_body(q_ref, k_ref, v_ref, qg_ref, o_ref):
    nt = (((1,), (1,)), ((), ()))
    for hd in range(MEM_HEADS):
        cols = slice(hd * MEM_HEAD_DIM, (hd + 1) * MEM_HEAD_DIM)
        q = q_ref[:, cols].astype(F32)
        qms = jnp.mean(q * q, axis=-1, keepdims=True)
        qn = (q * lax.rsqrt(qms + EPS) * qg_ref[...]).astype(BF16)
        s = lax.dot_general(qn, k_ref[:, cols], nt, preferred_element_type=F32)
        s = s * (MEM_HEAD_DIM ** -0.5)
        e = jnp.exp(s - jnp.max(s, axis=-1, keepdims=True))
        p = (e / jnp.sum(e, axis=-1, keepdims=True)).astype(BF16)
        o_ref[:, cols] = jnp.dot(p, v_ref[:, cols],
                                 preferred_element_type=F32).astype(o_ref.dtype)


def _mem_attn(proj, mk, mv, qg, tm, q_col):
    s = proj.shape[0]
    width = MEM_HEADS * MEM_HEAD_DIM
    m = mk.shape[0]
    return pl.pallas_call(
        _mem_attn_body,
        grid=(s // tm,),
        in_specs=[pl.BlockSpec((tm, width), lambda i: (i, q_col // width)),
                  pl.BlockSpec((m, width), lambda i: (0, 0)),
                  pl.BlockSpec((m, width), lambda i: (0, 0)),
                  pl.BlockSpec((1, MEM_HEAD_DIM), lambda i: (0, 0))],
        out_specs=pl.BlockSpec((tm, width), lambda i: (i, 0)),
        out_shape=jax.ShapeDtypeStruct((s, width), BF16),
        compiler_params=_params(1),
        name="mem_attn",
    )(proj, mk, mv, qg)


def _mix_out_body(x_ref, at_ref, cv_ref, mm_ref, ga0, ga1, gc0, gc1, gm0, gm1,
                  wa_ref, wc_ref, wm_ref, wo_ref, o_ref):
    at = at_ref[...]
    cv = cv_ref[...]
    mm = mm_ref[...]
    acc = x_ref[...]
    half = wo_ref.shape[0] // 2
    for c, (ga, gc, gm) in enumerate(((ga0, gc0, gm0), (ga1, gc1, gm1))):
        cols = slice(c * half, (c + 1) * half)
        merged = jax.nn.sigmoid(ga[...].astype(F32)) * jnp.dot(
            at, wa_ref[:, cols], preferred_element_type=F32)
        merged += jax.nn.sigmoid(gc[...].astype(F32)) * jnp.dot(
            cv, wc_ref[:, cols], preferred_element_type=F32)
        merged += jax.nn.sigmoid(gm[...].astype(F32)) * jnp.dot(
            mm, wm_ref[:, cols], preferred_element_type=F32)
        acc = acc + jnp.dot(merged.astype(BF16), wo_ref[cols, :], preferred_element_type=F32)
    o_ref[...] = acc


def _mix_out(x, attn, conv, memo, proj, wa, wc, wm, wo, tm, gate_col):
    s, d = x.shape
    half = d // 2
    g0 = gate_col // half
    gate = lambda k: pl.BlockSpec((tm, half), lambda i: (i, g0 + k))
    full = lambda a: pl.BlockSpec(a.shape, lambda i: (0, 0))
    return pl.pallas_call(
        _mix_out_body,
        grid=(s // tm,),
        in_specs=[pl.BlockSpec((tm, d), lambda i: (i, 0)),
                  pl.BlockSpec((tm, attn.shape[1]), lambda i: (i, 0)),
                  pl.BlockSpec((tm, conv.shape[1]), lambda i: (i, 0)),
                  pl.BlockSpec((tm, memo.shape[1]), lambda i: (i, 0)),
                  gate(0), gate(1), gate(2), gate(3), gate(4), gate(5),
                  full(wa), full(wc), full(wm), full(wo)],
        out_specs=pl.BlockSpec((tm, d), lambda i: (i, 0)),
        out_shape=jax.ShapeDtypeStruct((s, d), F32),
        compiler_params=_params(1),
        name="mix_out",
    )(x, attn, conv, memo, proj, proj, proj, proj, proj, proj, wa, wc, wm, wo)


def _rank_code(r):
    return -(RANK_BASE + r * RANK_STEP)


def _top16(s, key_index=None):
    vals = []
    for r in range(PEER_TOPK):
        m = jnp.max(s, axis=0, keepdims=True)
        hit = s == m
        if key_index is not None:
            first = jnp.min(jnp.where(hit, key_index, float(N_KEYS)), axis=0, keepdims=True)
            hit = key_index == first
        s = jnp.where(hit, _rank_code(r), s)
        vals.append(m)
    return vals, s


def _extracted(coded):
    return jnp.sum(jnp.where(coded < -0.5 * RANK_BASE, 1.0, 0.0), axis=0, keepdims=True)


def _pair_counts(v1, v2):
    shape = (SUBLANES, v1[0].shape[1])
    row = lax.broadcasted_iota(jnp.int32, shape, 0)
    rowf = row.astype(F32)
    v2lo = jnp.concatenate(v2[:8], axis=0)
    v2hi = jnp.concatenate(v2[8:], axis=0)
    ninf = jnp.full(shape, -jnp.inf, F32)

    def shifted(k):
        return pltpu.roll(v2lo, k, 0)

    groups = [v1[0] + v2lo, v1[0] + v2hi, v1[1] + v2lo,
              jnp.where(row < 5, v1[2] + v2lo, v1[4] + shifted(5)),
              jnp.where(row < 4, v1[3] + v2lo,
                        jnp.where(row < 6, v1[5] + shifted(4), v1[6] + shifted(6))),
              jnp.where(row < 2, v1[7] + v2lo, ninf),
              jnp.concatenate(v1[8:], axis=0) + v2[0]]
    index = [rowf, rowf + 8.0, rowf + 16.0,
             jnp.where(row < 5, rowf + 32.0, rowf + (64.0 - 5.0)),
             jnp.where(row < 4, rowf + 48.0,
                       jnp.where(row < 6, rowf + (80.0 - 4.0), rowf + (96.0 - 6.0))),
             rowf + 112.0,
             16.0 * (rowf + 8.0)]
    cand = jnp.concatenate(groups, axis=0)
    idx = jnp.concatenate(index, axis=0)
    remaining = cand
    best = v1[0] + v2[0]
    z = jnp.zeros_like(best)
    for _ in range(PEER_TOPK):
        m = jnp.max(remaining, axis=0, keepdims=True)
        first = jnp.min(jnp.where(remaining == m, idx, 4096.0), axis=0, keepdims=True)
        remaining = jnp.where(idx == first, -jnp.inf, remaining)
        z = z + jnp.exp(m - best)
    taken = jnp.where(remaining != cand, 1.0, 0.0)

    def count(group, lo, hi):
        rows = taken[SUBLANES * group:SUBLANES * (group + 1)]
        return jnp.sum(jnp.where((row >= lo) & (row < hi), rows, 0.0), axis=0, keepdims=True)

    cnt = [jnp.sum(taken[0:16], axis=0, keepdims=True), count(2, 0, 8),
           count(3, 0, 5), count(4, 0, 4), count(3, 5, 8), count(4, 4, 6), count(4, 6, 8),
           count(5, 0, 2)]
    cnt += [taken[48 + r:49 + r] for r in range(8)]
    return cnt, z


def _peer_route_body(x_ref, g_ref, wq_ref, keys_ref, ht_ref, cnt_ref, e1_ref, rk_ref, e2_ref,
                     q_ref, excess_ref, *, tf):
    x = x_ref[...]
    ms = jnp.mean(x * x, axis=-1, keepdims=True)
    ht = (x * lax.rsqrt(ms + EPS) * g_ref[...]).T.astype(BF16)
    ht_ref[...] = ht
    q_ref[...] = jnp.dot(wq_ref[...], ht, preferred_element_type=F32).astype(BF16)

    def head(hd, tie_safe):
        def q_rows(side):
            start = (2 * hd + side) * N_KEYS
            if not isinstance(hd, int):
                start = pl.multiple_of(start, N_KEYS)
            return q_ref[pl.ds(start, N_KEYS), :]

        s1 = jnp.dot(keys_ref[2 * hd], q_rows(0), preferred_element_type=F32)
        s2 = jnp.dot(keys_ref[2 * hd + 1], q_rows(1), preferred_element_type=F32)
        key_index = None
        if tie_safe:
            key_index = lax.broadcasted_iota(jnp.int32, (N_KEYS, LANES), 0).astype(F32)
        excess = jnp.zeros((1, LANES), F32)
        for c in range(tf // LANES):
            cols = slice(c * LANES, (c + 1) * LANES)
            s1c = s1[:, cols]
            s2c = s2[:, cols]
            v1, coded1 = _top16(s1c, key_index)
            v2, coded2 = _top16(s2c, key_index)
            cnt, z = _pair_counts(v1, v2)
            cnt_i = jnp.zeros_like(coded1)
            for r in range(PEER_TOPK):
                cnt_i = jnp.where(coded1 == _rank_code(r), cnt[r], cnt_i)
            rank2 = jnp.where(coded2 < -0.5 * RANK_BASE,
                              (-coded2 - RANK_BASE) * (1.0 / RANK_STEP), UNRANKED)
            cnt_ref[hd, :, cols] = cnt_i
            e1_ref[hd, :, cols] = jnp.exp(s1c - v1[0]) / z
            rk_ref[hd, :, cols] = rank2.astype(BF16)
            e2_ref[hd, :, cols] = jnp.exp(s2c - v2[0]).astype(BF16)
            if not tie_safe:
                excess = jnp.maximum(
                    excess, _extracted(coded1) + _extracted(coded2) - 2.0 * PEER_TOPK)
        return excess

    excess_ref[...] = jnp.concatenate([head(hd, False) for hd in range(PEER_HEADS)], axis=0)

    @pl.when(jnp.max(excess_ref[...]) > 0.0)
    def _():
        def redo(hd, carry):
            @pl.when(jnp.max(excess_ref[pl.ds(hd, 1), :]) > 0.0)
            def _():
                head(hd, True)
            return carry
        lax.fori_loop(0, PEER_HEADS, redo, 0)


def _peer_route(x1, g, wq_t, keys, tf):
    s, d = x1.shape
    qw = wq_t.shape[0]
    tab = lambda dt: jax.ShapeDtypeStruct((PEER_HEADS, N_KEYS, s), dt)
    tab_spec = pl.BlockSpec((PEER_HEADS, N_KEYS, tf), lambda i: (0, 0, i))
    return pl.pallas_call(
        functools.partial(_peer_route_body, tf=tf),
        grid=(s // tf,),
        in_specs=[pl.BlockSpec((tf, d), lambda i: (i, 0)),
                  pl.BlockSpec((1, d), lambda i: (0, 0)),
                  pl.BlockSpec((qw, d), lambda i: (0, 0)),
                  pl.BlockSpec(keys.shape, lambda i: (0, 0, 0))],
        out_specs=(pl.BlockSpec((d, tf), lambda i: (0, i)),
                   tab_spec, tab_spec, tab_spec, tab_spec),
        out_shape=(jax.ShapeDtypeStruct((d, s), BF16), tab(F32), tab(F32), tab(BF16), tab(BF16)),
        scratch_shapes=[pltpu.VMEM((qw, tf), BF16), pltpu.VMEM((PEER_HEADS, LANES), F32)],
        compiler_params=_params(1),
        name="peer_route",
    )(x1, g, wq_t, keys)


def _expert_vt_body(v_ref, o_ref):
    o_ref[0] = v_ref[...].T.astype(o_ref.dtype)


def _expert_vt(v, ec):
    n_exp, d = v.shape
    return pl.pallas_call(
        _expert_vt_body,
        grid=(n_exp // ec,),
        in_specs=[pl.BlockSpec((ec, d), lambda c: (c, 0))],
        out_specs=pl.BlockSpec((1, d, ec), lambda c: (c, 0, 0)),
        out_shape=jax.ShapeDtypeStruct((n_exp // ec, d, ec), BF16),
        compiler_params=_params(1),
        name="expert_vt",
    )(v)


def _peer_experts_body(ht_ref, u_ref, vt_ref, cnt_ref, e1_ref, rk_ref, e2_ref, x_ref, o_ref,
                       acc_ref, *, rows_per_chunk, n_parts):
    c = pl.program_id(1)
    tm = ht_ref.shape[1]
    rows_per_part = rows_per_chunk // n_parts

    @pl.when(c == 0)
    def _():
        acc_ref[...] = jnp.zeros_like(acc_ref)

    ht = ht_ref[...]

    def part_rows(p):
        return slice(p * rows_per_part * N_KEYS, (p + 1) * rows_per_part * N_KEYS)

    def project(p):
        return jnp.dot(u_ref[part_rows(p), :], ht, preferred_element_type=F32)

    def activate(p, a):
        acts = []
        for k in range(rows_per_part):
            r = p * rows_per_part + k
            ar = a[k * N_KEYS:(k + 1) * N_KEYS]
            gel = (0.5 * ar * (1.0 + lax.erf(ar * (2.0 ** -0.5)))).astype(BF16)
            gate = None
            for hd in range(PEER_HEADS):
                cnt = jnp.broadcast_to(cnt_ref[hd, r:r + 1, :], (N_KEYS, tm)).astype(BF16)
                e1 = jnp.broadcast_to(e1_ref[hd, r:r + 1, :], (N_KEYS, tm)).astype(BF16)
                e2 = e2_ref[hd]
                term = jnp.where(rk_ref[hd] < cnt, e2, jnp.zeros_like(e2)) * e1
                gate = term if gate is None else gate + term
            acts.append(gel * gate)
        return jnp.concatenate(acts, axis=0)

    act = jnp.concatenate([activate(p, project(p)) for p in range(n_parts)], axis=0)
    acc_ref[...] += jnp.dot(vt_ref[...], act, preferred_element_type=F32)

    @pl.when(c == pl.num_programs(1) - 1)
    def _():
        o_ref[...] = x_ref[...] + acc_ref[...].T


def _peer_experts(ht, u, vt, cnt, e1, rk, e2, x1, tm, ec, n_parts):
    s, d = x1.shape
    n_exp = u.shape[0]
    rpc = ec // N_KEYS
    tab = pl.BlockSpec((PEER_HEADS, N_KEYS, tm), lambda i, c: (0, 0, i))
    row = pl.BlockSpec((PEER_HEADS, rpc, tm), lambda i, c: (0, c, i))
    return pl.pallas_call(
        functools.partial(_peer_experts_body, rows_per_chunk=rpc, n_parts=n_parts),
        grid=(s // tm, n_exp // ec),
        in_specs=[pl.BlockSpec((d, tm), lambda i, c: (0, i)),
                  pl.BlockSpec((ec, d), lambda i, c: (c, 0)),
                  pl.BlockSpec((None, d, ec), lambda i, c: (c, 0, 0)),
                  row, row, tab, tab,
                  pl.BlockSpec((tm, d), lambda i, c: (i, 0))],
        out_specs=pl.BlockSpec((tm, d), lambda i, c: (i, 0)),
        out_shape=jax.ShapeDtypeStruct((s, d), F32),
        scratch_shapes=[pltpu.VMEM((d, tm), F32)],
        compiler_params=_params(2),
        name="peer_experts",
    )(ht, u, vt, cnt, e1, rk, e2, x1)


class _Tiles(NamedTuple):
    in_proj_m: int
    in_proj_n: int
    mixer: int
    route: int
    experts_m: int
    experts_chunk: int
    experts_parts: int


def _tiles(s):
    return _Tiles(in_proj_m=min(s, 1024), in_proj_n=1536, mixer=min(s, 512),
                  route=min(s, 256), experts_m=min(s, 512), experts_chunk=1024,
                  experts_parts=8)


def kernel(x, mem, positions, g_mix, w_in, q_norm_g, k_norm_g, attn_sinks, w_attn_o,
           conv_dw_w, conv_dw_b, conv_ln_g, conv_ln_b, w_conv_o, g_mem, w_mem_kv,
           mq_norm_g, mk_norm_g, w_mem_o, w_out, g_ffn, w_query, sub_keys, expert_u, expert_v):
    b, s, d = x.shape
    depth = g_mix.shape[0]
    q_w = N_HEADS * HEAD_DIM
    kv_w = N_KV_HEADS * HEAD_DIM
    conv_ch = conv_dw_b.shape[-1]
    mem_w = MEM_HEADS * MEM_HEAD_DIM
    conv_col = q_w + 2 * kv_w
    mq_col = conv_col + 2 * conv_ch
    gate_col = mq_col + mem_w
    t = _tiles(s)

    row = lambda v: v.reshape(1, -1).astype(F32)
    inv_freq = ROPE_THETA ** (-jnp.arange(0, HEAD_DIM, 2, dtype=F32) / HEAD_DIM)
    invf = jnp.tile(inv_freq, LANES // (HEAD_DIM // 2)).reshape(1, LANES)
    head_of_lane = jnp.arange(LANES) // HEAD_DIM
    bd = (head_of_lane[:, None] == head_of_lane[None, :]).astype(BF16) * (1.0 / HEAD_DIM)

    outs = []
    for bi in range(b):
        xb = x[bi]
        pos_b = jnp.broadcast_to(positions[bi].astype(F32)[:, None], (s, LANES))
        for l in range(depth):
            proj = _in_proj(xb, row(g_mix[l]), w_in[l].astype(BF16), t.in_proj_m, t.in_proj_n)
            attn = _swa(proj, pos_b, invf, jnp.tile(row(q_norm_g[l]), (1, 2)),
                        jnp.tile(row(k_norm_g[l]), (1, 2)), bd, attn_sinks[l].astype(F32),
                        t.mixer)
            conv = _conv(proj, conv_dw_w[l].reshape(CONV_WIDTH, conv_ch), row(conv_dw_b[l]),
                         row(conv_ln_g[l]), row(conv_ln_b[l]), t.mixer, conv_col, conv_ch)
            mk, mv = _mem_kv(mem[bi], row(g_mem[l]), w_mem_kv[l].astype(BF16),
                             row(mk_norm_g[l]))
            memo = _mem_attn(proj, mk, mv, row(mq_norm_g[l]), t.mixer, mq_col)
            x1 = _mix_out(xb, attn, conv, memo, proj, w_attn_o[l].astype(BF16),
                          w_conv_o[l].astype(BF16), w_mem_o[l].astype(BF16),
                          w_out[l].astype(BF16), t.mixer, gate_col)
            keys = sub_keys[l].reshape(2 * PEER_HEADS, N_KEYS, -1).astype(BF16)
            ht, cnt, e1, rk, e2 = _peer_route(x1, row(g_ffn[l]), w_query[l].astype(BF16).T,
                                              keys, t.route)
            xb = _peer_experts(ht, expert_u[l].astype(BF16),
                               _expert_vt(expert_v[l], t.experts_chunk),
                               cnt, e1, rk, e2, x1, t.experts_m, t.experts_chunk,
                               t.experts_parts)
        outs.append(xb)
    return jnp.stack(outs)
```

```python
import functools
from typing import NamedTuple

import jax
import jax.numpy as jnp
from jax import lax
from jax.experimental import pallas as pl
from jax.experimental.pallas import tpu as pltpu

F32 = jnp.float32
BF16 = jnp.bfloat16

EPS = 1e-6
NEG = -1e30
LANES = 128
SUBLANES = 8
HEAD_DIM = 64
N_HEADS = 16
N_KV_HEADS = 4
ATTN_BLOCK = 128
ROPE_THETA = 10000.0
CONV_WIDTH = 31
CONV_HALO = 32
MEM_HEADS = 4
MEM_HEAD_DIM = 128
PEER_HEADS = 8
N_KEYS = 128
PEER_TOPK = 16
UNRANKED = 64.0
RANK_BASE = 2.0 ** 100
RANK_STEP = 2.0 ** 77

VMEM_LIMIT = 56 * 1024 * 1024


def _params(n_axes, vmem=VMEM_LIMIT, **kwargs):
    return pltpu.CompilerParams(
        dimension_semantics=("arbitrary",) * n_axes, vmem_limit_bytes=vmem, **kwargs)


def _in_proj_body(x_ref, g_ref, w_ref, o_ref, h_ref):
    @pl.when(pl.program_id(1) == 0)
    def _():
        x = x_ref[...]
        ms = jnp.mean(x * x, axis=-1, keepdims=True)
        h_ref[...] = (x * lax.rsqrt(ms + EPS) * g_ref[...]).astype(BF16)

    o_ref[...] = jnp.dot(h_ref[...], w_ref[...],
                         preferred_element_type=F32).astype(o_ref.dtype)


def _in_proj(x, g, w, tm, tn):
    s, d = x.shape
    n = w.shape[1]
    return pl.pallas_call(
        _in_proj_body,
        grid=(s // tm, n // tn),
        in_specs=[pl.BlockSpec((tm, d), lambda i, j: (i, 0)),
                  pl.BlockSpec((1, d), lambda i, j: (0, 0)),
                  pl.BlockSpec((d, tn), lambda i, j: (0, j))],
        out_specs=pl.BlockSpec((tm, tn), lambda i, j: (i, j)),
        out_shape=jax.ShapeDtypeStruct((s, n), BF16),
        scratch_shapes=[pltpu.VMEM((tm, d), BF16)],
        compiler_params=_params(2),
        name="in_proj",
    )(x, g, w)


def _swa_body(sinks_ref, q_ref, kv_ref, kvp_ref, pos_ref, posp_ref, invf_ref, gq_ref,
              gk_ref, bd_ref, o_ref, klo_ref, khi_ref, vlo_ref, vhi_ref, bias_ref, *, tq):
    i = pl.program_id(0)
    nblk = tq // ATTN_BLOCK
    lane = lax.broadcasted_iota(jnp.int32, (1, LANES), 1)
    first_half = (lane % HEAD_DIM) < (HEAD_DIM // 2)
    low_head = lane < HEAD_DIM
    invf = invf_ref[...]
    bd = bd_ref[...]

    def rope_tables(pos):
        ang = pos * invf
        sin = jnp.sin(ang)
        return jnp.cos(ang), jnp.where(first_half, -sin, sin)

    def norm_rope(xf, g, cos, sin_signed):
        ms = jnp.dot((xf * xf).astype(BF16), bd, preferred_element_type=F32)
        xn = xf * lax.rsqrt(ms + EPS) * g
        swapped = jnp.where(first_half, pltpu.roll(xn, LANES - HEAD_DIM // 2, 1),
                            pltpu.roll(xn, HEAD_DIM // 2, 1))
        return xn * cos + swapped * sin_signed

    cos_c, sin_c = rope_tables(pos_ref[...])
    cos_p, sin_p = rope_tables(posp_ref[...])
    gk = gk_ref[...]

    def stage_kv(kv_tile, cos, sin_signed, row0, rows):
        for kc in range(N_KV_HEADS // 2):
            kf = kv_tile[:, kc * LANES:(kc + 1) * LANES].astype(F32)
            kr = norm_rope(kf, gk, cos, sin_signed)
            vf = kv_tile[:, (2 + kc) * LANES:(3 + kc) * LANES].astype(F32)
            for src, lo_ref, hi_ref in ((kr, klo_ref, khi_ref), (vf, vlo_ref, vhi_ref)):
                rolled = pltpu.roll(src, HEAD_DIM, 1)
                zero = jnp.zeros_like(src)
                sl = pl.ds(row0, rows)
                lo_ref[2 * kc, sl, :] = jnp.where(low_head, src, zero).astype(BF16)
                hi_ref[2 * kc, sl, :] = jnp.where(low_head, zero, rolled).astype(BF16)
                lo_ref[2 * kc + 1, sl, :] = jnp.where(low_head, rolled, zero).astype(BF16)
                hi_ref[2 * kc + 1, sl, :] = jnp.where(low_head, zero, src).astype(BF16)

    stage_kv(kvp_ref[...], cos_p, sin_p, 0, ATTN_BLOCK)
    stage_kv(kv_ref[...], cos_c, sin_c, ATTN_BLOCK, tq)

    qi = lax.broadcasted_iota(jnp.int32, (tq, 2 * ATTN_BLOCK), 0)
    kj = lax.broadcasted_iota(jnp.int32, (tq, 2 * ATTN_BLOCK), 1)
    diff = qi % ATTN_BLOCK + ATTN_BLOCK - kj
    first_key = jnp.where(i > 0, 0, ATTN_BLOCK)
    visible = ((diff >= 0) & (diff < ATTN_BLOCK)
               & ((kj >= first_key) | (qi >= ATTN_BLOCK)))
    bias_ref[...] = jnp.where(visible, 0.0, NEG)
    gq = gq_ref[...]
    ones = jnp.ones((2 * ATTN_BLOCK, LANES), BF16)
    nt = (((1,), (1,)), ((), ()))
    blocks = [(slice(n * ATTN_BLOCK, (n + 1) * ATTN_BLOCK),
               pl.ds(n * ATTN_BLOCK, 2 * ATTN_BLOCK)) for n in range(nblk)]

    for pc in range(N_HEADS // 2):
        g = pc // 2
        qf = q_ref[:, pc * LANES:(pc + 1) * LANES].astype(F32)
        qr = (norm_rope(qf, gq, cos_c, sin_c) * (HEAD_DIM ** -0.5)).astype(BF16)
        out = None
        for hh, (k_ref, v_ref) in enumerate(((klo_ref, vlo_ref), (khi_ref, vhi_ref))):
            s = jnp.concatenate(
                [lax.dot_general(qr[rows], k_ref[g, keys, :], nt, preferred_element_type=F32)
                 for rows, keys in blocks], axis=0) + bias_ref[...]
            sink = sinks_ref[2 * pc + hh]
            m = jnp.maximum(jnp.max(s, axis=-1, keepdims=True), sink)
            e = jnp.exp(s - m).astype(BF16)
            den = jnp.dot(e, ones, preferred_element_type=F32) + jnp.exp(sink - m)
            o = jnp.concatenate(
                [jnp.dot(e[rows], v_ref[g, keys, :], preferred_element_type=F32)
                 for rows, keys in blocks], axis=0) / den
            out = o if out is None else out + o
        o_ref[:, pc * LANES:(pc + 1) * LANES] = out.astype(o_ref.dtype)


def _swa(proj, pos_b, invf, gq, gk, bd, sinks, tq):
    s = proj.shape[0]
    rpb = tq // ATTN_BLOCK
    kv_blk = (N_HEADS * HEAD_DIM) // (2 * N_KV_HEADS * HEAD_DIM)
    prev = lambda i: jnp.maximum(i * rpb - 1, 0)
    rows = tq + ATTN_BLOCK
    stage = pltpu.VMEM((N_KV_HEADS, rows, LANES), BF16)
    return pl.pallas_call(
        functools.partial(_swa_body, tq=tq),
        grid=(s // tq,),
        in_specs=[pl.BlockSpec(memory_space=pltpu.SMEM),
                  pl.BlockSpec((tq, N_HEADS * HEAD_DIM), lambda i: (i, 0)),
                  pl.BlockSpec((tq, 2 * N_KV_HEADS * HEAD_DIM), lambda i: (i, kv_blk)),
                  pl.BlockSpec((ATTN_BLOCK, 2 * N_KV_HEADS * HEAD_DIM),
                               lambda i: (prev(i), kv_blk)),
                  pl.BlockSpec((tq, LANES), lambda i: (i, 0)),
                  pl.BlockSpec((ATTN_BLOCK, LANES), lambda i: (prev(i), 0)),
                  pl.BlockSpec((1, LANES), lambda i: (0, 0)),
                  pl.BlockSpec((1, LANES), lambda i: (0, 0)),
                  pl.BlockSpec((1, LANES), lambda i: (0, 0)),
                  pl.BlockSpec((LANES, LANES), lambda i: (0, 0))],
        out_specs=pl.BlockSpec((tq, N_HEADS * HEAD_DIM), lambda i: (i, 0)),
        out_shape=jax.ShapeDtypeStruct((s, N_HEADS * HEAD_DIM), BF16),
        scratch_shapes=[stage, stage, stage, stage,
                        pltpu.VMEM((tq, 2 * ATTN_BLOCK), F32)],
        compiler_params=_params(1),
        name="swa",
    )(sinks, proj, proj, proj, pos_b, pos_b, invf, gq, gk, bd)


def _conv_body(a_ref, b_ref, ah_ref, bh_ref, w_ref, b0_ref, lg_ref, lb_ref, o_ref,
               ext_ref, shift_ref, c_ref, *, tm):
    i = pl.program_id(0)
    ext_ref[CONV_HALO:, :] = a_ref[...].astype(F32) * jax.nn.sigmoid(b_ref[...].astype(F32))
    halo = ah_ref[...].astype(F32) * jax.nn.sigmoid(bh_ref[...].astype(F32))
    ext_ref[:CONV_HALO, :] = jnp.where(i > 0, halo, jnp.zeros_like(halo))
    span = shift_ref.shape[1]
    for k in range(1, SUBLANES):
        shift_ref[k] = ext_ref[k:k + span, :]
    ch = ext_ref.shape[1]
    rows = 128
    for cc in range(ch // LANES):
        cols = slice(cc * LANES, (cc + 1) * LANES)
        for rc in range(tm // rows):
            acc = jnp.broadcast_to(b0_ref[:, cols], (rows, LANES))
            for w in range(CONV_WIDTH):
                start = rc * rows + CONV_HALO - (CONV_WIDTH - 1) + w
                k = start % SUBLANES
                if k == 0:
                    window = ext_ref[start:start + rows, cols]
                else:
                    window = shift_ref[k, start - k:start - k + rows, cols]
                acc = acc + window * w_ref[w:w + 1, cols]
            c_ref[rc * rows:(rc + 1) * rows, cols] = acc
    c = c_ref[...]
    mu = jnp.mean(c, axis=-1, keepdims=True)
    xc = c - mu
    y = xc * lax.rsqrt(jnp.mean(xc * xc, axis=-1, keepdims=True) + EPS)
    y = y * lg_ref[...] + lb_ref[...]
    o_ref[...] = (y * jax.nn.sigmoid(y)).astype(o_ref.dtype)


def _conv(proj, dw_w, dw_b, ln_g, ln_b, tm, a_col, ch):
    s = proj.shape[0]
    a_blk = a_col // ch
    hpb = tm // CONV_HALO
    prev = lambda i: jnp.maximum(i * hpb - 1, 0)
    vec = pl.BlockSpec((1, ch), lambda i: (0, 0))
    return pl.pallas_call(
        functools.partial(_conv_body, tm=tm),
        grid=(s // tm,),
        in_specs=[pl.BlockSpec((tm, ch), lambda i: (i, a_blk)),
                  pl.BlockSpec((tm, ch), lambda i: (i, a_blk + 1)),
                  pl.BlockSpec((CONV_HALO, ch), lambda i: (prev(i), a_blk)),
                  pl.BlockSpec((CONV_HALO, ch), lambda i: (prev(i), a_blk + 1)),
                  pl.BlockSpec((CONV_WIDTH, ch), lambda i: (0, 0)),
                  vec, vec, vec],
        out_specs=pl.BlockSpec((tm, ch), lambda i: (i, 0)),
        out_shape=jax.ShapeDtypeStruct((s, ch), BF16),
        scratch_shapes=[pltpu.VMEM((tm + CONV_HALO, ch), F32),
                        pltpu.VMEM((SUBLANES, tm + CONV_HALO - SUBLANES, ch), F32),
                        pltpu.VMEM((tm, ch), F32)],
        compiler_params=_params(1),
        name="conv",
    )(proj, proj, proj, proj, dw_w, dw_b, ln_g, ln_b)


def _mem_kv_body(mem_ref, g_ref, w_ref, kg_ref, k_ref, v_ref):
    m = mem_ref[...]
    ms = jnp.mean(m * m, axis=-1, keepdims=True)
    h = (m * lax.rsqrt(ms + EPS) * g_ref[...]).astype(BF16)
    kv = jnp.dot(h, w_ref[...], preferred_element_type=F32)
    width = MEM_HEADS * MEM_HEAD_DIM
    for hd in range(MEM_HEADS):
        cols = slice(hd * MEM_HEAD_DIM, (hd + 1) * MEM_HEAD_DIM)
        k = kv[:, cols]
        kms = jnp.mean(k * k, axis=-1, keepdims=True)
        k_ref[:, cols] = (k * lax.rsqrt(kms + EPS) * kg_ref[...]).astype(BF16)
    v_ref[...] = kv[:, width:].astype(BF16)


def _mem_kv(mem, g, w, kg):
    m = mem.shape[0]
    width = MEM_HEADS * MEM_HEAD_DIM
    out = jax.ShapeDtypeStruct((m, width), BF16)
    return pl.pallas_call(
        _mem_kv_body,
        out_shape=(out, out),
        compiler_params=pltpu.CompilerParams(vmem_limit_bytes=VMEM_LIMIT),
        name="mem_kv",
    )(mem, g, w, kg)


def _mem_attn_body(q_ref, k_ref, v_ref, qg_ref, o_ref):
    nt = (((1,), (1,)), ((), ()))
    for hd in range(MEM_HEADS):
        cols = slice(hd * MEM_HEAD_DIM, (hd + 1) * MEM_HEAD_DIM)
        q = q_ref[:, cols].astype(F32)
        qms = jnp.mean(q * q, axis=-1, keepdims=True)
        qn = (q * lax.rsqrt(qms + EPS) * qg_ref[...]).astype(BF16)
        s = lax.dot_general(qn, k_ref[:, cols], nt, preferred_element_type=F32)
        s = s * (MEM_HEAD_DIM ** -0.5)
        e = jnp.exp(s - jnp.max(s, axis=-1, keepdims=True))
        p = (e / jnp.sum(e, axis=-1, keepdims=True)).astype(BF16)
        o_ref[:, cols] = jnp.dot(p, v_ref[:, cols],
                                 preferred_element_type=F32).astype(o_ref.dtype)


def _mem_attn(proj, mk, mv, qg, tm, q_col):
    s = proj.shape[0]
    width = MEM_HEADS * MEM_HEAD_DIM
    m = mk.shape[0]
    return pl.pallas_call(
        _mem_attn_body,
        grid=(s // tm,),
        in_specs=[pl.BlockSpec((tm, width), lambda i: (i, q_col // width)),
                  pl.BlockSpec((m, width), lambda i: (0, 0)),
                  pl.BlockSpec((m, width), lambda i: (0, 0)),
                  pl.BlockSpec((1, MEM_HEAD_DIM), lambda i: (0, 0))],
        out_specs=pl.BlockSpec((tm, width), lambda i: (i, 0)),
        out_shape=jax.ShapeDtypeStruct((s, width), BF16),
        compiler_params=_params(1),
        name="mem_attn",
    )(proj, mk, mv, qg)


def _mix_out_body(x_ref, at_ref, cv_ref, mm_ref, ga0, ga1, gc0, gc1, gm0, gm1,
                  wa_ref, wc_ref, wm_ref, wo_ref, o_ref):
    at = at_ref[...]
    cv = cv_ref[...]
    mm = mm_ref[...]
    acc = x_ref[...]
    half = wo_ref.shape[0] // 2
    for c, (ga, gc, gm) in enumerate(((ga0, gc0, gm0), (ga1, gc1, gm1))):
        cols = slice(c * half, (c + 1) * half)
        merged = jax.nn.sigmoid(ga[...].astype(F32)) * jnp.dot(
            at, wa_ref[:, cols], preferred_element_type=F32)
        merged += jax.nn.sigmoid(gc[...].astype(F32)) * jnp.dot(
            cv, wc_ref[:, cols], preferred_element_type=F32)
        merged += jax.nn.sigmoid(gm[...].astype(F32)) * jnp.dot(
            mm, wm_ref[:, cols], preferred_element_type=F32)
        acc = acc + jnp.dot(merged.astype(BF16), wo_ref[cols, :], preferred_element_type=F32)
    o_ref[...] = acc


def _mix_out(x, attn, conv, memo, proj, wa, wc, wm, wo, tm, gate_col):
    s, d = x.shape
    half = d // 2
    g0 = gate_col // half
    gate = lambda k: pl.BlockSpec((tm, half), lambda i: (i, g0 + k))
    full = lambda a: pl.BlockSpec(a.shape, lambda i: (0, 0))
    return pl.pallas_call(
        _mix_out_body,
        grid=(s // tm,),
        in_specs=[pl.BlockSpec((tm, d), lambda i: (i, 0)),
                  pl.BlockSpec((tm, attn.shape[1]), lambda i: (i, 0)),
                  pl.BlockSpec((tm, conv.shape[1]), lambda i: (i, 0)),
                  pl.BlockSpec((tm, memo.shape[1]), lambda i: (i, 0)),
                  gate(0), gate(1), gate(2), gate(3), gate(4), gate(5),
                  full(wa), full(wc), full(wm), full(wo)],
        out_specs=pl.BlockSpec((tm, d), lambda i: (i, 0)),
        out_shape=jax.ShapeDtypeStruct((s, d), F32),
        compiler_params=_params(1),
        name="mix_out",
    )(x, attn, conv, memo, proj, proj, proj, proj, proj, proj, wa, wc, wm, wo)


def _rank_code(r):
    return -(RANK_BASE + r * RANK_STEP)


def _top16(s, key_index=None):
    vals = []
    for r in range(PEER_TOPK):
        m = jnp.max(s, axis=0, keepdims=True)
        hit = s == m
        if key_index is not None:
            first = jnp.min(jnp.where(hit, key_index, float(N_KEYS)), axis=0, keepdims=True)
            hit = key_index == first
        s = jnp.where(hit, _rank_code(r), s)
        vals.append(m)
    return vals, s


def _extracted(coded):
    return jnp.sum(jnp.where(coded < -0.5 * RANK_BASE, 1.0, 0.0), axis=0, keepdims=True)


def _pair_counts(v1, v2):
    shape = (SUBLANES, v1[0].shape[1])
    row = lax.broadcasted_iota(jnp.int32, shape, 0)
    rowf = row.astype(F32)
    v2lo = jnp.concatenate(v2[:8], axis=0)
    v2hi = jnp.concatenate(v2[8:], axis=0)
    ninf = jnp.full(shape, -jnp.inf, F32)

    def shifted(k):
        return pltpu.roll(v2lo, k, 0)

    groups = [v1[0] + v2lo, v1[0] + v2hi, v1[1] + v2lo,
              jnp.where(row < 5, v1[2] + v2lo, v1[4] + shifted(5)),
              jnp.where(row < 4, v1[3] + v2lo,
                        jnp.where(row < 6, v1[5] + shifted(4), v1[6] + shifted(6))),
              jnp.where(row < 2, v1[7] + v2lo, ninf),
              jnp.concatenate(v1[8:], axis=0) + v2[0]]
    index = [rowf, rowf + 8.0, rowf + 16.0,
             jnp.where(row < 5, rowf + 32.0, rowf + (64.0 - 5.0)),
             jnp.where(row < 4, rowf + 48.0,
                       jnp.where(row < 6, rowf + (80.0 - 4.0), rowf + (96.0 - 6.0))),
             rowf + 112.0,
             16.0 * (rowf + 8.0)]
    cand = jnp.concatenate(groups, axis=0)
    idx = jnp.concatenate(index, axis=0)
    remaining = cand
    best = v1[0] + v2[0]
    z = jnp.zeros_like(best)
    for _ in range(PEER_TOPK):
        m = jnp.max(remaining, axis=0, keepdims=True)
        first = jnp.min(jnp.where(remaining == m, idx, 4096.0), axis=0, keepdims=True)
        remaining = jnp.where(idx == first, -jnp.inf, remaining)
        z = z + jnp.exp(m - best)
    taken = jnp.where(remaining != cand, 1.0, 0.0)

    def count(group, lo, hi):
        rows = taken[SUBLANES * group:SUBLANES * (group + 1)]
        return jnp.sum(jnp.where((row >= lo) & (row < hi), rows, 0.0), axis=0, keepdims=True)

    cnt = [jnp.sum(taken[0:16], axis=0, keepdims=True), count(2, 0, 8),
           count(3, 0, 5), count(4, 0, 4), count(3, 5, 8), count(4, 4, 6), count(4, 6, 8),
           count(5, 0, 2)]
    cnt += [taken[48 + r:49 + r] for r in range(8)]
    return cnt, z


def _peer_route_body(x_ref, g_ref, wq_ref, keys_ref, ht_ref, cnt_ref, e1_ref, rk_ref, e2_ref,
                     q_ref, excess_ref, *, tf):
    x = x_ref[...]
    ms = jnp.mean(x * x, axis=-1, keepdims=True)
    ht = (x * lax.rsqrt(ms + EPS) * g_ref[...]).T.astype(BF16)
    ht_ref[...] = ht
    q_ref[...] = jnp.dot(wq_ref[...], ht, preferred_element_type=F32).astype(BF16)

    def head(hd, tie_safe):
        def q_rows(side):
            start = (2 * hd + side) * N_KEYS
            if not isinstance(hd, int):
                start = pl.multiple_of(start, N_KEYS)
            return q_ref[pl.ds(start, N_KEYS), :]

        s1 = jnp.dot(keys_ref[2 * hd], q_rows(0), preferred_element_type=F32)
        s2 = jnp.dot(keys_ref[2 * hd + 1], q_rows(1), preferred_element_type=F32)
        key_index = None
        if tie_safe:
            key_index = lax.broadcasted_iota(jnp.int32, (N_KEYS, LANES), 0).astype(F32)
        excess = jnp.zeros((1, LANES), F32)
        for c in range(tf // LANES):
            cols = slice(c * LANES, (c + 1) * LANES)
            s1c = s1[:, cols]
            s2c = s2[:, cols]
            v1, coded1 = _top16(s1c, key_index)
            v2, coded2 = _top16(s2c, key_index)
            cnt, z = _pair_counts(v1, v2)
            cnt_i = jnp.zeros_like(coded1)
            for r in range(PEER_TOPK):
                cnt_i = jnp.where(coded1 == _rank_code(r), cnt[r], cnt_i)
            rank2 = jnp.where(coded2 < -0.5 * RANK_BASE,
                              (-coded2 - RANK_BASE) * (1.0 / RANK_STEP), UNRANKED)
            cnt_ref[hd, :, cols] = cnt_i
            e1_ref[hd, :, cols] = jnp.exp(s1c - v1[0]) / z
            rk_ref[hd, :, cols] = rank2.astype(BF16)
            e2_ref[hd, :, cols] = jnp.exp(s2c - v2[0]).astype(BF16)
            if not tie_safe:
                excess = jnp.maximum(
                    excess, _extracted(coded1) + _extracted(coded2) - 2.0 * PEER_TOPK)
        return excess

    excess_ref[...] = jnp.concatenate([head(hd, False) for hd in range(PEER_HEADS)], axis=0)

    @pl.when(jnp.max(excess_ref[...]) > 0.0)
    def _():
        def redo(hd, carry):
            @pl.when(jnp.max(excess_ref[pl.ds(hd, 1), :]) > 0.0)
            def _():
                head(hd, True)
            return carry
        lax.fori_loop(0, PEER_HEADS, redo, 0)


def _peer_route(x1, g, wq_t, keys, tf):
    s, d = x1.shape
    qw = wq_t.shape[0]
    tab = lambda dt: jax.ShapeDtypeStruct((PEER_HEADS, N_KEYS, s), dt)
    tab_spec = pl.BlockSpec((PEER_HEADS, N_KEYS, tf), lambda i: (0, 0, i))
    return pl.pallas_call(
        functools.partial(_peer_route_body, tf=tf),
        grid=(s // tf,),
        in_specs=[pl.BlockSpec((tf, d), lambda i: (i, 0)),
                  pl.BlockSpec((1, d), lambda i: (0, 0)),
                  pl.BlockSpec((qw, d), lambda i: (0, 0)),
                  pl.BlockSpec(keys.shape, lambda i: (0, 0, 0))],
        out_specs=(pl.BlockSpec((d, tf), lambda i: (0, i)),
                   tab_spec, tab_spec, tab_spec, tab_spec),
        out_shape=(jax.ShapeDtypeStruct((d, s), BF16), tab(F32), tab(F32), tab(BF16), tab(BF16)),
        scratch_shapes=[pltpu.VMEM((qw, tf), BF16), pltpu.VMEM((PEER_HEADS, LANES), F32)],
        compiler_params=_params(1),
        name="peer_route",
    )(x1, g, wq_t, keys)


def _expert_vt_body(v_ref, o_ref):
    o_ref[0] = v_ref[...].T.astype(o_ref.dtype)


def _expert_vt(v, ec):
    n_exp, d = v.shape
    return pl.pallas_call(
        _expert_vt_body,
        grid=(n_exp // ec,),
        in_specs=[pl.BlockSpec((ec, d), lambda c: (c, 0))],
        out_specs=pl.BlockSpec((1, d, ec), lambda c: (c, 0, 0)),
        out_shape=jax.ShapeDtypeStruct((n_exp // ec, d, ec), BF16),
        compiler_params=_params(1),
        name="expert_vt",
    )(v)


def _peer_experts_body(ht_ref, u_ref, vt_ref, cnt_ref, e1_ref, rk_ref, e2_ref, x_ref, o_ref,
                       acc_ref, *, rows_per_chunk, n_parts):
    c = pl.program_id(1)
    tm = ht_ref.shape[1]
    rows_per_part = rows_per_chunk // n_parts

    @pl.when(c == 0)
    def _():
        acc_ref[...] = jnp.zeros_like(acc_ref)

    ht = ht_ref[...]

    def part_rows(p):
        return slice(p * rows_per_part * N_KEYS, (p + 1) * rows_per_part * N_KEYS)

    def project(p):
        return jnp.dot(u_ref[part_rows(p), :], ht, preferred_element_type=F32)

    def activate(p, a):
        acts = []
        for k in range(rows_per_part):
            r = p * rows_per_part + k
            ar = a[k * N_KEYS:(k + 1) * N_KEYS]
            gel = (0.5 * ar * (1.0 + lax.erf(ar * (2.0 ** -0.5)))).astype(BF16)
            gate = None
            for hd in range(PEER_HEADS):
                cnt = jnp.broadcast_to(cnt_ref[hd, r:r + 1, :], (N_KEYS, tm)).astype(BF16)
                e1 = jnp.broadcast_to(e1_ref[hd, r:r + 1, :], (N_KEYS, tm)).astype(BF16)
                e2 = e2_ref[hd]
                term = jnp.where(rk_ref[hd] < cnt, e2, jnp.zeros_like(e2)) * e1
                gate = term if gate is None else gate + term
            acts.append(gel * gate)
        return jnp.concatenate(acts, axis=0)

    act = jnp.concatenate([activate(p, project(p)) for p in range(n_parts)], axis=0)
    acc_ref[...] += jnp.dot(vt_ref[...], act, preferred_element_type=F32)

    @pl.when(c == pl.num_programs(1) - 1)
    def _():
        o_ref[...] = x_ref[...] + acc_ref[...].T


def _peer_experts(ht, u, vt, cnt, e1, rk, e2, x1, tm, ec, n_parts):
    s, d = x1.shape
    n_exp = u.shape[0]
    rpc = ec // N_KEYS
    tab = pl.BlockSpec((PEER_HEADS, N_KEYS, tm), lambda i, c: (0, 0, i))
    row = pl.BlockSpec((PEER_HEADS, rpc, tm), lambda i, c: (0, c, i))
    return pl.pallas_call(
        functools.partial(_peer_experts_body, rows_per_chunk=rpc, n_parts=n_parts),
        grid=(s // tm, n_exp // ec),
        in_specs=[pl.BlockSpec((d, tm), lambda i, c: (0, i)),
                  pl.BlockSpec((ec, d), lambda i, c: (c, 0)),
                  pl.BlockSpec((None, d, ec), lambda i, c: (c, 0, 0)),
                  row, row, tab, tab,
                  pl.BlockSpec((tm, d), lambda i, c: (i, 0))],
        out_specs=pl.BlockSpec((tm, d), lambda i, c: (i, 0)),
        out_shape=jax.ShapeDtypeStruct((s, d), F32),
        scratch_shapes=[pltpu.VMEM((d, tm), F32)],
        compiler_params=_params(2),
        name="peer_experts",
    )(ht, u, vt, cnt, e1, rk, e2, x1)


class _Tiles(NamedTuple):
    in_proj_m: int
    in_proj_n: int
    swa: int
    mixer: int
    route: int
    experts_m: int
    experts_chunk: int
    experts_parts: int


def _tiles(s):
    return _Tiles(in_proj_m=min(s, 1024), in_proj_n=2304, swa=min(s, 1024), mixer=min(s, 512),
                  route=min(s, 256), experts_m=min(s, 512), experts_chunk=1024,
                  experts_parts=8)


def kernel(x, mem, positions, g_mix, w_in, q_norm_g, k_norm_g, attn_sinks, w_attn_o,
           conv_dw_w, conv_dw_b, conv_ln_g, conv_ln_b, w_conv_o, g_mem, w_mem_kv,
           mq_norm_g, mk_norm_g, w_mem_o, w_out, g_ffn, w_query, sub_keys, expert_u, expert_v):
    b, s, d = x.shape
    depth = g_mix.shape[0]
    q_w = N_HEADS * HEAD_DIM
    kv_w = N_KV_HEADS * HEAD_DIM
    conv_ch = conv_dw_b.shape[-1]
    mem_w = MEM_HEADS * MEM_HEAD_DIM
    conv_col = q_w + 2 * kv_w
    mq_col = conv_col + 2 * conv_ch
    gate_col = mq_col + mem_w
    t = _tiles(s)

    row = lambda v: v.reshape(1, -1).astype(F32)
    inv_freq = ROPE_THETA ** (-jnp.arange(0, HEAD_DIM, 2, dtype=F32) / HEAD_DIM)
    invf = jnp.tile(inv_freq, LANES // (HEAD_DIM // 2)).reshape(1, LANES)
    head_of_lane = jnp.arange(LANES) // HEAD_DIM
    bd = (head_of_lane[:, None] == head_of_lane[None, :]).astype(BF16) * (1.0 / HEAD_DIM)

    outs = []
    for bi in range(b):
        xb = x[bi]
        pos_b = jnp.broadcast_to(positions[bi].astype(F32)[:, None], (s, LANES))
        for l in range(depth):
            proj = _in_proj(xb, row(g_mix[l]), w_in[l].astype(BF16), t.in_proj_m, t.in_proj_n)
            attn = _swa(proj, pos_b, invf, jnp.tile(row(q_norm_g[l]), (1, 2)),
                        jnp.tile(row(k_norm_g[l]), (1, 2)), bd, attn_sinks[l].astype(F32),
                        t.swa)
            conv = _conv(proj, conv_dw_w[l].reshape(CONV_WIDTH, conv_ch), row(conv_dw_b[l]),
                         row(conv_ln_g[l]), row(conv_ln_b[l]), t.mixer, conv_col, conv_ch)
            mk, mv = _mem_kv(mem[bi], row(g_mem[l]), w_mem_kv[l].astype(BF16),
                             row(mk_norm_g[l]))
            memo = _mem_attn(proj, mk, mv, row(mq_norm_g[l]), t.mixer, mq_col)
            x1 = _mix_out(xb, attn, conv, memo, proj, w_attn_o[l].astype(BF16),
                          w_conv_o[l].astype(BF16), w_mem_o[l].astype(BF16),
                          w_out[l].astype(BF16), t.mixer, gate_col)
            keys = sub_keys[l].reshape(2 * PEER_HEADS, N_KEYS, -1).astype(BF16)
            ht, cnt, e1, rk, e2 = _peer_route(x1, row(g_ffn[l]), w_query[l].astype(BF16).T,
                                              keys, t.route)
            xb = _peer_experts(ht, expert_u[l].astype(BF16),
                               _expert_vt(expert_v[l], t.experts_chunk),
                               cnt, e1, rk, e2, x1, t.experts_m, t.experts_chunk,
                               t.experts_parts)
        outs.append(xb)
    return jnp.stack(outs)
```
